```python
import jax, jax.numpy as jnp
from jax import lax
import numpy as np

D_MODEL = 1024
BATCH = 8
SEQ = 2048
DEPTH = 2
DEC_BATCH = 128
DEC_SEQ = 1
PAST_LEN = 16384
PAGE_SIZE = 128

D_POOL = D_MODEL // 2
POOL_WINDOWS = (2, 4, 8, 16)
N_POOL_GROUPS = len(POOL_WINDOWS)
POOL_GROUP = D_POOL // N_POOL_GROUPS
POOL_HIST = max(POOL_WINDOWS) - 1
D_RNN = D_MODEL // 2
RNN_HEADS = 8
RNN_HEAD_DIM = D_RNN // RNN_HEADS
CONV_WIDTH = 4
RGLRU_C = 8.0
D_MIX = D_POOL + D_RNN
D_IN = D_POOL + 2 * D_RNN
N_MEM = 256
MEM_HEADS = 4
MEM_HEAD_DIM = D_MODEL // MEM_HEADS
D_FF = 2816
N_EXPERTS = 8
TOP_K = 2
D_FF_EXPERT = D_FF // TOP_K
N_DENSE = (DEPTH + 1) // 2
N_MOE = DEPTH // 2
EPS = 1e-6

kernel_name = 'hybrid_pool_rglru_decoder_step'


def rmsnorm(x, g):
    xf = x.astype(jnp.float32)
    y = xf * lax.rsqrt(jnp.mean(xf * xf, axis=-1, keepdims=True) + EPS)
    return (y * g.astype(jnp.float32)).astype(x.dtype)


def pool_mixer(u, hist, pos, w_blk, scale):
    B, T, _ = u.shape
    full = jnp.concatenate([hist, u], axis=1).astype(jnp.float32)
    cs = lax.cumsum(full, axis=1)
    cs = jnp.concatenate([jnp.zeros_like(cs[:, :1]), cs], axis=1)
    end = cs[:, POOL_HIST + 1:]
    outs = []
    for g, w in enumerate(POOL_WINDOWS):
        sl = slice(g * POOL_GROUP, (g + 1) * POOL_GROUP)
        start = cs[:, POOL_HIST + 1 - w:POOL_HIST + 1 - w + T, sl]
        cnt = jnp.minimum(pos + 1, w).astype(jnp.float32)[None, :, None]
        outs.append((end[..., sl] - start) / cnt)
    z = jnp.concatenate(outs, axis=-1) - u.astype(jnp.float32)
    z = z.reshape(B, T, N_POOL_GROUPS, POOL_GROUP)
    z = jnp.einsum('btgc,gcd->btgd', z, w_blk.astype(jnp.float32)).reshape(B, T, D_POOL)
    y = z * scale.astype(jnp.float32)
    return y.astype(u.dtype), full[:, -POOL_HIST:].astype(u.dtype)


def causal_conv(x, buf, w, b):
    T = x.shape[1]
    full = jnp.concatenate([buf.astype(x.dtype), x], axis=1)
    y = b.astype(x.dtype)
    for k in range(CONV_WIDTH):
        y = y + full[:, k:k + T] * w[k].astype(x.dtype)
    return y, full[:, -(CONV_WIDTH - 1):]


def _scan_combine(left, right):
    a1, b1 = left
    a2, b2 = right
    return a1 * a2, a2 * b1 + b2


def rglru(x, h0, pos, wa, ba, wx, bx, lam):
    B, T, _ = x.shape
    f32 = jnp.float32
    xf = x.astype(f32)
    xh = xf.reshape(B, T, RNN_HEADS, RNN_HEAD_DIM)
    r = jax.nn.sigmoid(jnp.einsum('bthi,hij->bthj', xh, wa.astype(f32)).reshape(B, T, D_RNN) + ba.astype(f32))
    i = jax.nn.sigmoid(jnp.einsum('bthi,hij->bthj', xh, wx.astype(f32)).reshape(B, T, D_RNN) + bx.astype(f32))
    log_a = -RGLRU_C * r * jax.nn.softplus(-lam.astype(f32))
    a = jnp.exp(log_a)
    mult = jnp.sqrt(-jnp.expm1(2.0 * log_a))
    mult = jnp.where((pos == 0)[None, :, None], 1.0, mult)
    bterm = mult * (i * xf)
    bterm = bterm.at[:, 0].add(a[:, 0] * h0.astype(f32))
    _, h = lax.associative_scan(_scan_combine, (a, bterm), axis=1)
    return h, h[:, -1]


def mem_kv(mem, g, wk, wv):
    B = mem.shape[0]
    mn = rmsnorm(mem, g)
    k = (mn @ wk).reshape(B, N_MEM, MEM_HEADS, MEM_HEAD_DIM)
    v = (mn @ wv).reshape(B, N_MEM, MEM_HEADS, MEM_HEAD_DIM)
    return k, v


def mem_attend(xn, k, v, wq, wo):
    B, T, _ = xn.shape
    q = (xn @ wq).reshape(B, T, MEM_HEADS, MEM_HEAD_DIM)
    s = jnp.einsum('bthd,bmhd->bhtm', q, k.astype(q.dtype)).astype(jnp.float32) * (MEM_HEAD_DIM ** -0.5)
    p = jax.nn.softmax(s, axis=-1).astype(xn.dtype)
    o = jnp.einsum('bhtm,bmhd->bthd', p, v.astype(xn.dtype)).reshape(B, T, D_MODEL)
    return o @ wo


def swiglu(x, wg, wu, wd):
    return (jax.nn.silu(x @ wg) * (x @ wu)) @ wd


def moe(x, router_w, wg, wu, wd):
    B, T, D = x.shape
    xt = x.reshape(B * T, D)
    logits = (xt @ router_w).astype(jnp.float32)
    top_v, top_i = lax.top_k(logits, TOP_K)
    gates = jax.nn.softmax(top_v, axis=-1)
    combine = jnp.sum(jax.nn.one_hot(top_i, N_EXPERTS, dtype=jnp.float32) * gates[..., None], axis=1)
    out = jnp.zeros_like(xt)
    for e in range(N_EXPERTS):
        out = out + combine[:, e:e + 1].astype(x.dtype) * swiglu(xt, wg[e], wu[e], wd[e])
    return out.reshape(B, T, D)


def _layer(x, l, pos, pool_hist, conv_buf, h0, mk, mv, P):
    hn = rmsnorm(x, P['norm_mix'][l])
    proj = hn @ P['w_in'][l]
    u_pool = proj[..., :D_POOL]
    u_rnn = proj[..., D_POOL:D_MIX]
    u_gate = proj[..., D_MIX:]
    y_pool, new_hist = pool_mixer(u_pool, pool_hist, pos, P['pool_w'][l], P['pool_scale'][l])
    c, new_conv = causal_conv(u_rnn, conv_buf, P['conv_w'][l], P['conv_b'][l])
    h, h_last = rglru(c, h0, pos, P['gate_a_w'][l], P['gate_a_b'][l], P['gate_x_w'][l], P['gate_x_b'][l], P['rglru_lambda'][l])
    y_rnn = (h * jax.nn.gelu(u_gate.astype(jnp.float32))).astype(x.dtype)
    mixed = jnp.concatenate([rmsnorm(y_pool, P['out_norm_pool'][l]), rmsnorm(y_rnn, P['out_norm_rnn'][l])], axis=-1)
    x = x + mixed @ P['w_out'][l]
    x = x + mem_attend(rmsnorm(x, P['norm_mem'][l]), mk, mv, P['w_mq'][l], P['w_mo'][l])
    hn = rmsnorm(x, P['norm_ffn'][l])
    j = l // 2
    if l % 2 == 0:
        x = x + swiglu(hn, P['ffn_w_gate'][j], P['ffn_w_up'][j], P['ffn_w_down'][j])
    else:
        x = x + moe(hn, P['router_w'][j], P['moe_w_gate'][j], P['moe_w_up'][j], P['moe_w_down'][j])
    return x, new_hist, new_conv, h_last.astype(x.dtype)


def _trunk(x, pos, pool_hist, conv_buf, h0, mem_k, mem_v, P):
    hists, convs, hs = [], [], []
    for l in range(DEPTH):
        x, nh, nc, hl = _layer(x, l, pos, pool_hist[l], conv_buf[l], h0[l], mem_k[l], mem_v[l], P)
        hists.append(nh)
        convs.append(nc)
        hs.append(hl)
    return rmsnorm(x, P['final_norm']), jnp.stack(hists), jnp.stack(convs), jnp.stack(hs)


def setup_inputs(seed: int = 0) -> dict:
    key = jax.random.key(seed)
    ks = iter(jax.random.split(key, 48))
    f32 = jnp.float32
    L = DEPTH

    def nrm(shape, scale=1.0):
        return scale * jax.random.normal(next(ks), shape, f32)

    def gain(shape):
        return 1.0 + 0.05 * jax.random.normal(next(ks), shape, f32)

    x_prompt = nrm((BATCH, SEQ, D_MODEL))
    x_sample = nrm((DEC_BATCH, DEC_SEQ, D_MODEL))
    mem_prompt = nrm((BATCH, N_MEM, D_MODEL))
    state_pool = nrm((L, DEC_BATCH, POOL_HIST, D_POOL))
    state_conv = nrm((L, DEC_BATCH, CONV_WIDTH - 1, D_RNN))
    state_h = nrm((L, DEC_BATCH, D_RNN), 0.5)
    cache_mem_k = nrm((L, DEC_BATCH, N_MEM, MEM_HEADS, MEM_HEAD_DIM))
    cache_mem_v = nrm((L, DEC_BATCH, N_MEM, MEM_HEADS, MEM_HEAD_DIM))
    norm_mix = gain((L, D_MODEL))
    w_in = nrm((L, D_MODEL, D_IN), D_MODEL ** -0.5)
    pool_w = nrm((L, N_POOL_GROUPS, POOL_GROUP, POOL_GROUP), POOL_GROUP ** -0.5)
    pool_scale = gain((L, D_POOL))
    conv_w = nrm((L, CONV_WIDTH, D_RNN), CONV_WIDTH ** -0.5)
    conv_b = nrm((L, D_RNN), 0.02)
    gate_a_w = nrm((L, RNN_HEADS, RNN_HEAD_DIM, RNN_HEAD_DIM), RNN_HEAD_DIM ** -0.5)
    gate_a_b = nrm((L, D_RNN), 0.02)
    gate_x_w = nrm((L, RNN_HEADS, RNN_HEAD_DIM, RNN_HEAD_DIM), RNN_HEAD_DIM ** -0.5)
    gate_x_b = nrm((L, D_RNN), 0.02)
    u = jax.random.uniform(next(ks), (L, D_RNN), f32, 0.9 ** 2, 0.999 ** 2)
    s = -0.5 * jnp.log(u)
    rglru_lambda = -jnp.log(jnp.expm1(s))
    out_norm_pool = gain((L, D_POOL))
    out_norm_rnn = gain((L, D_RNN))
    w_out = nrm((L, D_MIX, D_MODEL), D_MIX ** -0.5)
    norm_mem = gain((L, D_MODEL))
    mem_norm = gain((L, D_MODEL))
    w_mq = nrm((L, D_MODEL, D_MODEL), D_MODEL ** -0.5)
    w_mk = nrm((L, D_MODEL, D_MODEL), D_MODEL ** -0.5)
    w_mv = nrm((L, D_MODEL, D_MODEL), D_MODEL ** -0.5)
    w_mo = nrm((L, D_MODEL, D_MODEL), D_MODEL ** -0.5)
    norm_ffn = gain((L, D_MODEL))
    ffn_w_gate = nrm((N_DENSE, D_MODEL, D_FF), D_MODEL ** -0.5)
    ffn_w_up = nrm((N_DENSE, D_MODEL, D_FF), D_MODEL ** -0.5)
    ffn_w_down = nrm((N_DENSE, D_FF, D_MODEL), D_FF ** -0.5)
    router_w = nrm((N_MOE, D_MODEL, N_EXPERTS), D_MODEL ** -0.5)
    moe_w_gate = nrm((N_MOE, N_EXPERTS, D_MODEL, D_FF_EXPERT), D_MODEL ** -0.5)
    moe_w_up = nrm((N_MOE, N_EXPERTS, D_MODEL, D_FF_EXPERT), D_MODEL ** -0.5)
    moe_w_down = nrm((N_MOE, N_EXPERTS, D_FF_EXPERT, D_MODEL), D_FF_EXPERT ** -0.5)
    final_norm = gain((D_MODEL,))
    return {'x_prompt': x_prompt, 'x_sample': x_sample, 'mem_prompt': mem_prompt,
            'state_pool': state_pool, 'state_conv': state_conv, 'state_h': state_h,
            'cache_mem_k': cache_mem_k, 'cache_mem_v': cache_mem_v,
            'norm_mix': norm_mix, 'w_in': w_in, 'pool_w': pool_w, 'pool_scale': pool_scale,
            'conv_w': conv_w, 'conv_b': conv_b, 'gate_a_w': gate_a_w, 'gate_a_b': gate_a_b,
            'gate_x_w': gate_x_w, 'gate_x_b': gate_x_b, 'rglru_lambda': rglru_lambda,
            'out_norm_pool': out_norm_pool, 'out_norm_rnn': out_norm_rnn, 'w_out': w_out,
            'norm_mem': norm_mem, 'mem_norm': mem_norm, 'w_mq': w_mq, 'w_mk': w_mk, 'w_mv': w_mv, 'w_mo': w_mo,
            'norm_ffn': norm_ffn, 'ffn_w_gate': ffn_w_gate, 'ffn_w_up': ffn_w_up, 'ffn_w_down': ffn_w_down,
            'router_w': router_w, 'moe_w_gate': moe_w_gate, 'moe_w_up': moe_w_up, 'moe_w_down': moe_w_down,
            'final_norm': final_norm}


def reference(x_prompt, x_sample, mem_prompt, state_pool, state_conv, state_h, cache_mem_k, cache_mem_v,
              norm_mix, w_in, pool_w, pool_scale, conv_w, conv_b, gate_a_w, gate_a_b, gate_x_w, gate_x_b,
              rglru_lambda, out_norm_pool, out_norm_rnn, w_out, norm_mem, mem_norm, w_mq, w_mk, w_mv, w_mo,
              norm_ffn, ffn_w_gate, ffn_w_up, ffn_w_down, router_w, moe_w_gate, moe_w_up, moe_w_down, final_norm):
    P = dict(norm_mix=norm_mix, w_in=w_in, pool_w=pool_w, pool_scale=pool_scale, conv_w=conv_w, conv_b=conv_b,
             gate_a_w=gate_a_w, gate_a_b=gate_a_b, gate_x_w=gate_x_w, gate_x_b=gate_x_b, rglru_lambda=rglru_lambda,
             out_norm_pool=out_norm_pool, out_norm_rnn=out_norm_rnn, w_out=w_out, norm_mem=norm_mem,
             w_mq=w_mq, w_mo=w_mo, norm_ffn=norm_ffn, ffn_w_gate=ffn_w_gate, ffn_w_up=ffn_w_up,
             ffn_w_down=ffn_w_down, router_w=router_w, moe_w_gate=moe_w_gate, moe_w_up=moe_w_up,
             moe_w_down=moe_w_down, final_norm=final_norm)
    B, T, _ = x_prompt.shape
    dt = x_prompt.dtype
    mks, mvs = [], []
    for l in range(DEPTH):
        k, v = mem_kv(mem_prompt, mem_norm[l], w_mk[l], w_mv[l])
        mks.append(k)
        mvs.append(v)
    mem_k_p = jnp.stack(mks)
    mem_v_p = jnp.stack(mvs)
    pos_p = jnp.arange(T, dtype=jnp.int32)
    zeros_pool = jnp.zeros((DEPTH, B, POOL_HIST, D_POOL), dt)
    zeros_conv = jnp.zeros((DEPTH, B, CONV_WIDTH - 1, D_RNN), dt)
    zeros_h = jnp.zeros((DEPTH, B, D_RNN), dt)
    y_prompt, pool_p, conv_p, h_p = _trunk(x_prompt, pos_p, zeros_pool, zeros_conv, zeros_h, mem_k_p, mem_v_p, P)
    pos_s = PAST_LEN + jnp.arange(x_sample.shape[1], dtype=jnp.int32)
    y_sample, pool_s, conv_s, h_s = _trunk(x_sample, pos_s, state_pool, state_conv, state_h, cache_mem_k, cache_mem_v, P)
    return (y_prompt, y_sample, pool_p, conv_p, h_p, mem_k_p, mem_v_p, pool_s, conv_s, h_s)
```

```python
import functools
import math

import jax
import jax.numpy as jnp
from jax import lax
from jax.experimental import pallas as pl
from jax.experimental.pallas import tpu as pltpu

F32 = jnp.float32
BF16 = jnp.bfloat16

D_MODEL = 1024
DEPTH = 2
PAST_LEN = 16384
D_POOL = 512
POOL_WINDOWS = (2, 4, 8, 16)
POOL_GROUP = 128
POOL_HIST = 15
D_RNN = 512
RNN_HEADS = 8
CONV_WIDTH = 4
RGLRU_C = 8.0
D_MIX = 1024
D_IN = 1536
N_MEM = 256
MEM_HEADS = 4
MEM_HEAD_DIM = 256
N_EXPERTS = 8
D_FF_CHUNK = 1408
EPS = 1e-6

SUBLANES = 8
LANES = 128
POOL_PAD = 16
CONV_PAD = 8
VMEM_LIMIT = 56 * 1024 * 1024


def _cparams(sem):
    return pltpu.CompilerParams(dimension_semantics=sem, vmem_limit_bytes=VMEM_LIMIT)


def _rms(x, g):
    ms = jnp.mean(x * x, axis=-1, keepdims=True)
    return x * lax.rsqrt(ms + EPS) * g


def _gelu_tanh(x):
    c = math.sqrt(2.0 / math.pi)
    return 0.5 * x * (1.0 + jnp.tanh(c * (x + 0.044715 * (x * x * x))))


def _softplus(x):
    return jnp.maximum(x, 0.0) + jnp.log1p(jnp.exp(-jnp.abs(x)))


def _norm_matmul_body(x_ref, g_ref, w_ref, o_ref):
    hn = _rms(x_ref[...], g_ref[...]).astype(BF16)
    o_ref[...] = jnp.dot(hn, w_ref[...], preferred_element_type=F32)


def _norm_matmul(x, g, w, tm):
    n, d = x.shape
    m = w.shape[1]
    return pl.pallas_call(
        _norm_matmul_body,
        grid=(n // tm,),
        in_specs=[pl.BlockSpec((tm, d), lambda i: (i, 0)),
                  pl.BlockSpec((1, d), lambda i: (0, 0)),
                  pl.BlockSpec((d, m), lambda i: (0, 0))],
        out_specs=pl.BlockSpec((tm, m), lambda i: (i, 0)),
        out_shape=jax.ShapeDtypeStruct((n, m), F32),
        compiler_params=_cparams(("parallel",)),
        name="norm_matmul",
    )(x, g, w)


def _matmul_res_body(a_ref, w_ref, r_ref, o_ref):
    o_ref[...] = r_ref[...] + jnp.dot(a_ref[...].astype(BF16), w_ref[...],
                                      preferred_element_type=F32)


def _matmul_res(a, w, res, tm):
    n, k = a.shape
    m = w.shape[1]
    return pl.pallas_call(
        _matmul_res_body,
        grid=(n // tm,),
        in_specs=[pl.BlockSpec((tm, k), lambda i: (i, 0)),
                  pl.BlockSpec((k, m), lambda i: (0, 0)),
                  pl.BlockSpec((tm, m), lambda i: (i, 0))],
        out_specs=pl.BlockSpec((tm, m), lambda i: (i, 0)),
        out_shape=jax.ShapeDtypeStruct((n, m), F32),
        compiler_params=_cparams(("parallel",)),
        name="matmul_res",
    )(a, w, res)


def _pool_project(z_groups, pw_ref, scale):
    ys = [jnp.dot(z.astype(BF16), pw_ref[g], preferred_element_type=F32)
          for g, z in enumerate(z_groups)]
    return jnp.concatenate(ys, axis=-1) * scale


def _rglru_terms(c, first_pos, wa_ref, ba_ref, wx_ref, bx_ref, lam_ref):
    cb = c.astype(BF16)
    r = jax.nn.sigmoid(jnp.dot(cb, wa_ref[...], preferred_element_type=F32) + ba_ref[...])
    i = jax.nn.sigmoid(jnp.dot(cb, wx_ref[...], preferred_element_type=F32) + bx_ref[...])
    log_a = (-RGLRU_C) * r * _softplus(-lam_ref[...])
    a = jnp.exp(log_a)
    mult = jnp.sqrt(1.0 - a * a)
    if first_pos is not None:
        mult = jnp.where(first_pos, 1.0, mult)
    return a, mult * (i * c)


def _mix_out(y_pool, h, u_gate, gp_ref, gr_ref):
    y_rnn = h * _gelu_tanh(u_gate)
    return jnp.concatenate([_rms(y_pool, gp_ref[...]), _rms(y_rnn, gr_ref[...])], axis=-1)


def _mixer_body(pos0, tt, proj_ref, hist_ref, conv_ref, h0_ref, pw_ref, ps_ref, cw_ref, cb_ref,
                wa_ref, ba_ref, wx_ref, bx_ref, lam_ref, gp_ref, gr_ref,
                mix_ref, hl_ref, pbuf, cbuf, a_s, b_s, hc):
    t = pl.program_id(1)

    @pl.when(t == 0)
    def _():
        pbuf[0:POOL_PAD, :] = hist_ref[...]
        cbuf[0:CONV_PAD, :] = conv_ref[...]
        hc[...] = jnp.broadcast_to(h0_ref[...], (SUBLANES, D_RNN))

    @pl.when(t > 0)
    def _():
        pbuf[0:POOL_PAD, :] = pbuf[tt:tt + POOL_PAD, :]
        cbuf[0:CONV_PAD, :] = cbuf[tt:tt + CONV_PAD, :]

    u_pool = proj_ref[:, 0:D_POOL]
    pbuf[POOL_PAD:POOL_PAD + tt, :] = u_pool
    cbuf[CONV_PAD:CONV_PAD + tt, :] = proj_ref[:, D_POOL:D_MIX]

    pos = pos0 + t * tt + lax.broadcasted_iota(jnp.int32, (tt, 1), 0)

    zs = []
    for g, w in enumerate(POOL_WINDOWS):
        sl = slice(g * POOL_GROUP, (g + 1) * POOL_GROUP)
        acc = pbuf[POOL_PAD:POOL_PAD + tt, sl]
        for k in range(1, w):
            acc = acc + pbuf[POOL_PAD - k:POOL_PAD - k + tt, sl]
        cnt = jnp.minimum(pos + 1, w).astype(F32)
        zs.append(acc / cnt - u_pool[:, sl])
    y_pool = _pool_project(zs, pw_ref, ps_ref[...])

    c = cb_ref[...]
    for k in range(CONV_WIDTH):
        off = CONV_PAD - (CONV_WIDTH - 1) + k
        c = c + cbuf[off:off + tt, :] * cw_ref[k:k + 1, :]

    a, b = _rglru_terms(c, pos == 0, wa_ref, ba_ref, wx_ref, bx_ref, lam_ref)
    a_s[...] = a
    b_s[...] = b

    row = lax.broadcasted_iota(jnp.int32, (SUBLANES, D_RNN), 0)

    def group(j, carry):
        r0 = pl.multiple_of(j * SUBLANES, SUBLANES)
        ga = a_s[pl.ds(r0, SUBLANES), :]
        gb = b_s[pl.ds(r0, SUBLANES), :]
        for s in (1, 2, 4):
            keep = row >= s
            gb = jnp.where(keep, ga * pltpu.roll(gb, s, axis=0) + gb, gb)
            ga = jnp.where(keep, ga * pltpu.roll(ga, s, axis=0), ga)
        h = ga * carry + gb
        b_s[pl.ds(r0, SUBLANES), :] = h
        return jnp.broadcast_to(h[SUBLANES - 1:SUBLANES, :], (SUBLANES, D_RNN))

    carry = lax.fori_loop(0, tt // SUBLANES, group, hc[...])
    hc[...] = carry
    hl_ref[...] = carry

    mix = _mix_out(y_pool, b_s[...], proj_ref[:, D_MIX:D_IN], gp_ref, gr_ref)
    mix_ref[...] = mix.astype(BF16)


def _mixer_prompt(proj, hist, conv, h0, lw, pos0, tt):
    bsz, t_len, _ = proj.shape
    full = lambda shape: pl.BlockSpec(shape, lambda b, t: (0,) * len(shape))
    per_b = lambda rows, cols: pl.BlockSpec((None, rows, cols), lambda b, t: (b, 0, 0))
    return pl.pallas_call(
        functools.partial(_mixer_body, pos0, tt),
        grid=(bsz, t_len // tt),
        in_specs=[pl.BlockSpec((None, tt, D_IN), lambda b, t: (b, t, 0)),
                  per_b(POOL_PAD, D_POOL), per_b(CONV_PAD, D_RNN), per_b(1, D_RNN),
                  full((len(POOL_WINDOWS), POOL_GROUP, POOL_GROUP)), full((1, D_POOL)),
                  full((CONV_WIDTH, D_RNN)), full((1, D_RNN)),
                  full((D_RNN, D_RNN)), full((1, D_RNN)),
                  full((D_RNN, D_RNN)), full((1, D_RNN)),
                  full((1, D_RNN)), full((1, D_POOL)), full((1, D_RNN))],
        out_specs=[pl.BlockSpec((None, tt, D_MIX), lambda b, t: (b, t, 0)),
                   per_b(SUBLANES, D_RNN)],
        out_shape=[jax.ShapeDtypeStruct((bsz, t_len, D_MIX), BF16),
                   jax.ShapeDtypeStruct((bsz, SUBLANES, D_RNN), F32)],
        scratch_shapes=[pltpu.VMEM((POOL_PAD + tt, D_POOL), F32),
                        pltpu.VMEM((CONV_PAD + tt, D_RNN), F32),
                        pltpu.VMEM((tt, D_RNN), F32),
                        pltpu.VMEM((tt, D_RNN), F32),
                        pltpu.VMEM((SUBLANES, D_RNN), F32)],
        compiler_params=_cparams(("parallel", "arbitrary")),
        name="mixer_prompt",
    )(proj, hist, conv, h0, lw["pool_w"], lw["pool_scale"], lw["conv_w"], lw["conv_b"],
      lw["gate_a_w"], lw["gate_a_b"], lw["gate_x_w"], lw["gate_x_b"], lw["lam"],
      lw["out_norm_pool"], lw["out_norm_rnn"])


def _mixer_decode_body(pos0, proj_ref, hist_ref, conv_ref, h0_ref, pw_ref, ps_ref, cw_ref, cb_ref,
                       wa_ref, ba_ref, wx_ref, bx_ref, lam_ref, gp_ref, gr_ref, mix_ref, hn_ref):
    u_pool = proj_ref[:, 0:D_POOL]
    zs = []
    for g, w in enumerate(POOL_WINDOWS):
        sl = slice(g * POOL_GROUP, (g + 1) * POOL_GROUP)
        acc = u_pool[:, sl]
        for k in range(1, w):
            acc = acc + hist_ref[POOL_HIST - k, :, sl]
        zs.append(acc / float(min(pos0 + 1, w)) - u_pool[:, sl])
    y_pool = _pool_project(zs, pw_ref, ps_ref[...])

    c = cb_ref[...]
    for k in range(CONV_WIDTH - 1):
        c = c + conv_ref[k] * cw_ref[k:k + 1, :]
    c = c + proj_ref[:, D_POOL:D_MIX] * cw_ref[CONV_WIDTH - 1:CONV_WIDTH, :]

    a, b = _rglru_terms(c, None, wa_ref, ba_ref, wx_ref, bx_ref, lam_ref)
    h = a * h0_ref[...] + b
    hn_ref[...] = h
    mix_ref[...] = _mix_out(y_pool, h, proj_ref[:, D_MIX:D_IN], gp_ref, gr_ref).astype(BF16)


def _mixer_decode(proj, hist_t, conv_t, h0, lw, pos0):
    assert pos0 > 0
    n = proj.shape[0]
    vm = pl.BlockSpec(memory_space=pltpu.VMEM)
    return pl.pallas_call(
        functools.partial(_mixer_decode_body, pos0),
        in_specs=[vm] * 15,
        out_specs=[vm, vm],
        out_shape=[jax.ShapeDtypeStruct((n, D_MIX), BF16),
                   jax.ShapeDtypeStruct((n, D_RNN), F32)],
        compiler_params=pltpu.CompilerParams(vmem_limit_bytes=VMEM_LIMIT),
        name="mixer_decode",
    )(proj, hist_t, conv_t, h0, lw["pool_w"], lw["pool_scale"], lw["conv_w"], lw["conv_b"],
      lw["gate_a_w"], lw["gate_a_b"], lw["gate_x_w"], lw["gate_x_b"], lw["lam"],
      lw["out_norm_pool"], lw["out_norm_rnn"])


def _softmax_rows(s):
    e = jnp.exp(s - jnp.max(s, axis=-1, keepdims=True))
    return e / jnp.sum(e, axis=-1, keepdims=True)


_NT = (((1,), (1,)), ((), ()))


def _attn_prompt_body(x_ref, k_ref, v_ref, g_ref, wq_ref, wo_ref, o_ref):
    x = x_ref[...]
    xn = _rms(x, g_ref[...]).astype(BF16)
    q = jnp.dot(xn, wq_ref[...], preferred_element_type=F32)
    outs = []
    for h in range(MEM_HEADS):
        sl = slice(h * MEM_HEAD_DIM, (h + 1) * MEM_HEAD_DIM)
        s = lax.dot_general(q[:, sl].astype(BF16), k_ref[:, sl].astype(BF16), _NT,
                            preferred_element_type=F32) * (MEM_HEAD_DIM ** -0.5)
        p = _softmax_rows(s).astype(BF16)
        outs.append(jnp.dot(p, v_ref[:, sl].astype(BF16), preferred_element_type=F32))
    o = jnp.concatenate(outs, axis=-1).astype(BF16)
    o_ref[...] = x + jnp.dot(o, wo_ref[...], preferred_element_type=F32)


def _attn_prompt(x, k, v, g, wq, wo, tt):
    bsz, t_len, d = x.shape
    full = lambda shape: pl.BlockSpec(shape, lambda b, t: (0,) * len(shape))
    kv = pl.BlockSpec((None, N_MEM, d), lambda b, t: (b, 0, 0))
    xs = pl.BlockSpec((None, tt, d), lambda b, t: (b, t, 0))
    return pl.pallas_call(
        _attn_prompt_body,
        grid=(bsz, t_len // tt),
        in_specs=[xs, kv, kv, full((1, d)), full((d, d)), full((d, d))],
        out_specs=xs,
        out_shape=jax.ShapeDtypeStruct(x.shape, F32),
        compiler_params=_cparams(("parallel", "parallel")),
        name="attn_prompt",
    )(x, k, v, g, wq, wo)


ATTN_ROWS = 16


def _attn_decode_body(bb, q_ref, k_ref, v_ref, o_ref):
    row = lax.broadcasted_iota(jnp.int32, (ATTN_ROWS, D_MODEL), 0)
    lane = lax.broadcasted_iota(jnp.int32, (ATTN_ROWS, D_MODEL), 1)
    own = (lane // MEM_HEAD_DIM) == row
    for b in range(bb):
        qm = jnp.where(own, q_ref[b:b + 1, :], 0.0).astype(BF16)
        s = lax.dot_general(qm, k_ref[b].astype(BF16), _NT,
                            preferred_element_type=F32) * (MEM_HEAD_DIM ** -0.5)
        p = _softmax_rows(s).astype(BF16)
        o = jnp.dot(p, v_ref[b].astype(BF16), preferred_element_type=F32)
        o_ref[b:b + 1, :] = jnp.sum(jnp.where(own, o, 0.0), axis=0, keepdims=True)


def _attn_decode(q, k, v, bb):
    n, d = q.shape
    kv = pl.BlockSpec((bb, N_MEM, d), lambda i: (i, 0, 0))
    qs = pl.BlockSpec((bb, d), lambda i: (i, 0))
    return pl.pallas_call(
        functools.partial(_attn_decode_body, bb),
        grid=(n // bb,),
        in_specs=[qs, kv, kv],
        out_specs=qs,
        out_shape=jax.ShapeDtypeStruct((n, d), F32),
        compiler_params=_cparams(("parallel",)),
        name="attn_decode",
    )(q, k, v)


def _router_body(x_ref, g_ref, rw_ref, cw_ref):
    hn = _rms(x_ref[...], g_ref[...])
    logits = jnp.dot(hn, rw_ref[...], precision=lax.Precision.HIGHEST,
                     preferred_element_type=F32)
    lane = lax.broadcasted_iota(jnp.int32, logits.shape, 1)
    neg = -jnp.inf
    lg = jnp.where(lane < N_EXPERTS, logits, neg)
    m1 = jnp.max(lg, axis=-1, keepdims=True)
    i1 = jnp.min(jnp.where(lg == m1, lane, LANES), axis=-1, keepdims=True)
    lg2 = jnp.where(lane == i1, neg, lg)
    m2 = jnp.max(lg2, axis=-1, keepdims=True)
    i2 = jnp.min(jnp.where(lg2 == m2, lane, LANES), axis=-1, keepdims=True)
    e2 = jnp.exp(m2 - m1)
    den = 1.0 + e2
    cw_ref[...] = jnp.where(lane == i1, 1.0 / den, 0.0) + jnp.where(lane == i2, e2 / den, 0.0)


def _router(x, g, rw, tm):
    n, d = x.shape
    return pl.pallas_call(
        _router_body,
        grid=(n // tm,),
        in_specs=[pl.BlockSpec((tm, d), lambda i: (i, 0)),
                  pl.BlockSpec((1, d), lambda i: (0, 0)),
                  pl.BlockSpec((d, LANES), lambda i: (0, 0))],
        out_specs=pl.BlockSpec((tm, LANES), lambda i: (i, 0)),
        out_shape=jax.ShapeDtypeStruct((n, LANES), F32),
        compiler_params=_cparams(("parallel",)),
        name="router",
    )(x, g, rw)


def _ffn_body(n_chunks, routed, final, *refs):
    x_ref, g_ref, wg_ref, wu_ref, wd_ref = refs[:5]
    rest = list(refs[5:])
    cw_ref = rest.pop(0) if routed else None
    fg_ref = rest.pop(0) if final else None
    o_ref, hn_s = rest
    j = pl.program_id(1)

    @pl.when(j == 0)
    def _():
        hn_s[...] = _rms(x_ref[...], g_ref[...]).astype(BF16)
        o_ref[...] = x_ref[...]

    hn = hn_s[...]
    gate = jnp.dot(hn, wg_ref[...], preferred_element_type=F32)
    up = jnp.dot(hn, wu_ref[...], preferred_element_type=F32)
    mid = (gate * jax.nn.sigmoid(gate) * up).astype(BF16)
    y = jnp.dot(mid, wd_ref[...], preferred_element_type=F32)
    if routed:
        cw = cw_ref[...]
        lane = lax.broadcasted_iota(jnp.int32, cw.shape, 1)
        y = y * jnp.sum(jnp.where(lane == j, cw, 0.0), axis=-1, keepdims=True)
    o_ref[...] += y

    if final:
        @pl.when(j == n_chunks - 1)
        def _():
            o_ref[...] = _rms(o_ref[...], fg_ref[...])


def _ffn(x, g, wg, wu, wd, tm, cw=None, final_g=None):
    n, d = x.shape
    routed = cw is not None
    final = final_g is not None
    fc = D_FF_CHUNK
    if routed:
        n_chunks = wg.shape[0]
        w_in_spec = pl.BlockSpec((None, d, fc), lambda i, j: (j, 0, 0))
        w_dn_spec = pl.BlockSpec((None, fc, d), lambda i, j: (j, 0, 0))
    else:
        n_chunks = wg.shape[1] // fc
        w_in_spec = pl.BlockSpec((d, fc), lambda i, j: (0, j))
        w_dn_spec = pl.BlockSpec((fc, d), lambda i, j: (j, 0))
    row = pl.BlockSpec((tm, d), lambda i, j: (i, 0))
    vec = pl.BlockSpec((1, d), lambda i, j: (0, 0))
    in_specs = [row, vec, w_in_spec, w_in_spec, w_dn_spec]
    args = [x, g, wg, wu, wd]
    if routed:
        in_specs.append(pl.BlockSpec((tm, LANES), lambda i, j: (i, 0)))
        args.append(cw)
    if final:
        in_specs.append(vec)
        args.append(final_g)
    return pl.pallas_call(
        functools.partial(_ffn_body, n_chunks, routed, final),
        grid=(n // tm, n_chunks),
        in_specs=in_specs,
        out_specs=row,
        out_shape=jax.ShapeDtypeStruct((n, d), F32),
        scratch_shapes=[pltpu.VMEM((tm, d), BF16)],
        compiler_params=_cparams(("parallel", "arbitrary")),
        name="ffn",
    )(*args)


def _block_diag(w):
    h, a, b = w.shape
    eye = jnp.eye(h, dtype=w.dtype)
    return (eye[:, None, :, None] * w[:, :, None, :]).reshape(h * a, h * b)


def _layer_weights(l, P):
    row = lambda v: v[l].reshape(1, -1)
    return dict(
        norm_mix=row(P["norm_mix"]), w_in=P["w_in"][l].astype(BF16),
        pool_w=P["pool_w"][l].astype(BF16), pool_scale=row(P["pool_scale"]),
        conv_w=P["conv_w"][l], conv_b=row(P["conv_b"]),
        gate_a_w=_block_diag(P["gate_a_w"][l]).astype(BF16), gate_a_b=row(P["gate_a_b"]),
        gate_x_w=_block_diag(P["gate_x_w"][l]).astype(BF16), gate_x_b=row(P["gate_x_b"]),
        lam=row(P["rglru_lambda"]),
        out_norm_pool=row(P["out_norm_pool"]), out_norm_rnn=row(P["out_norm_rnn"]),
        w_out=P["w_out"][l].astype(BF16), norm_mem=row(P["norm_mem"]),
        w_mq=P["w_mq"][l].astype(BF16), w_mo=P["w_mo"][l].astype(BF16),
        norm_ffn=row(P["norm_ffn"]))


def _channel_mixer(x, l, lw, P, tm, final_g):
    j = l // 2
    if l % 2 == 0:
        return _ffn(x, lw["norm_ffn"], P["ffn_w_gate"][j].astype(BF16),
                    P["ffn_w_up"][j].astype(BF16), P["ffn_w_down"][j].astype(BF16),
                    tm, final_g=final_g)
    rw = jnp.pad(P["router_w"][j], ((0, 0), (0, LANES - N_EXPERTS)))
    cw = _router(x, lw["norm_ffn"], rw, tm)
    return _ffn(x, lw["norm_ffn"], P["moe_w_gate"][j].astype(BF16),
                P["moe_w_up"][j].astype(BF16), P["moe_w_down"][j].astype(BF16),
                tm, cw=cw, final_g=final_g)


def kernel(x_prompt, x_sample, mem_prompt, state_pool, state_conv, state_h, cache_mem_k, cache_mem_v,
           norm_mix, w_in, pool_w, pool_scale, conv_w, conv_b, gate_a_w, gate_a_b, gate_x_w, gate_x_b,
           rglru_lambda, out_norm_pool, out_norm_rnn, w_out, norm_mem, mem_norm, w_mq, w_mk, w_mv, w_mo,
           norm_ffn, ffn_w_gate, ffn_w_up, ffn_w_down, router_w, moe_w_gate, moe_w_up, moe_w_down,
           final_norm):
    P = dict(norm_mix=norm_mix, w_in=w_in, pool_w=pool_w, pool_scale=pool_scale, conv_w=conv_w,
             conv_b=conv_b, gate_a_w=gate_a_w, gate_a_b=gate_a_b, gate_x_w=gate_x_w,
             gate_x_b=gate_x_b, rglru_lambda=rglru_lambda, out_norm_pool=out_norm_pool,
             out_norm_rnn=out_norm_rnn, w_out=w_out, norm_mem=norm_mem, w_mq=w_mq, w_mo=w_mo,
             norm_ffn=norm_ffn, ffn_w_gate=ffn_w_gate, ffn_w_up=ffn_w_up, ffn_w_down=ffn_w_down,
             router_w=router_w, moe_w_gate=moe_w_gate, moe_w_up=moe_w_up, moe_w_down=moe_w_down)
    bsz, t_len, d = x_prompt.shape
    n_dec = x_sample.shape[0]
    n_tok = bsz * t_len
    fg = final_norm.reshape(1, d)
    TM, TT, TM_DEC, BB = 512, 512, n_dec, SUBLANES

    xp = x_prompt
    xs = x_sample.reshape(n_dec, d)
    mem2 = mem_prompt.reshape(bsz * N_MEM, d)
    mem_ks, mem_vs, pools_p, convs_p, hs_p, pools_s, convs_s, hs_s = ([] for _ in range(8))
    zeros_hist = jnp.zeros((bsz, POOL_PAD, D_POOL), F32)
    zeros_conv = jnp.zeros((bsz, CONV_PAD, D_RNN), F32)
    zeros_h = jnp.zeros((bsz, 1, D_RNN), F32)

    for l in range(DEPTH):
        lw = _layer_weights(l, P)
        last_g = fg if l == DEPTH - 1 else None

        mg = mem_norm[l].reshape(1, d)
        mk = _norm_matmul(mem2, mg, w_mk[l].astype(BF16), TM).reshape(bsz, N_MEM, d)
        mv = _norm_matmul(mem2, mg, w_mv[l].astype(BF16), TM).reshape(bsz, N_MEM, d)
        mem_ks.append(mk.reshape(bsz, N_MEM, MEM_HEADS, MEM_HEAD_DIM))
        mem_vs.append(mv.reshape(bsz, N_MEM, MEM_HEADS, MEM_HEAD_DIM))

        proj = _norm_matmul(xp.reshape(n_tok, d), lw["norm_mix"], lw["w_in"], TM)
        proj = proj.reshape(bsz, t_len, D_IN)
        mixed, h_last = _mixer_prompt(proj, zeros_hist, zeros_conv, zeros_h, lw, 0, TT)
        pools_p.append(proj[:, t_len - POOL_HIST:, :D_POOL])
        convs_p.append(proj[:, t_len - (CONV_WIDTH - 1):, D_POOL:D_MIX])
        hs_p.append(h_last[:, 0, :])
        x2 = _matmul_res(mixed.reshape(n_tok, D_MIX), lw["w_out"], xp.reshape(n_tok, d), TM)
        x3 = _attn_prompt(x2.reshape(bsz, t_len, d), mk, mv, lw["norm_mem"], lw["w_mq"],
                          lw["w_mo"], TT)
        xp = _channel_mixer(x3.reshape(n_tok, d), l, lw, P, TM, last_g).reshape(bsz, t_len, d)

        proj_s = _norm_matmul(xs, lw["norm_mix"], lw["w_in"], TM_DEC)
        mixed_s, h_new = _mixer_decode(proj_s, jnp.swapaxes(state_pool[l], 0, 1),
                                       jnp.swapaxes(state_conv[l], 0, 1), state_h[l], lw,
                                       PAST_LEN)
        pools_s.append(jnp.concatenate([state_pool[l][:, 1:], proj_s[:, None, :D_POOL]], axis=1))
        convs_s.append(jnp.concatenate([state_conv[l][:, 1:], proj_s[:, None, D_POOL:D_MIX]],
                                       axis=1))
        hs_s.append(h_new)
        xs2 = _matmul_res(mixed_s, lw["w_out"], xs, TM_DEC)
        q_s = _norm_matmul(xs2, lw["norm_mem"], lw["w_mq"], TM_DEC)
        att = _attn_decode(q_s, cache_mem_k[l].reshape(n_dec, N_MEM, d),
                           cache_mem_v[l].reshape(n_dec, N_MEM, d), BB)
        xs3 = _matmul_res(att, lw["w_mo"], xs2, TM_DEC)
        xs = _channel_mixer(xs3, l, lw, P, TM_DEC, last_g)

    return (xp, xs.reshape(n_dec, 1, d), jnp.stack(pools_p), jnp.stack(convs_p), jnp.stack(hs_p),
            jnp.stack(mem_ks), jnp.stack(mem_vs), jnp.stack(pools_s), jnp.stack(convs_s),
            jnp.stack(hs_s))
```

```python
import functools
import math

import jax
import jax.numpy as jnp
from jax import lax
from jax.experimental import pallas as pl
from jax.experimental.pallas import tpu as pltpu

F32 = jnp.float32
BF16 = jnp.bfloat16

D_MODEL = 1024
DEPTH = 2
PAST_LEN = 16384
D_POOL = 512
POOL_WINDOWS = (2, 4, 8, 16)
POOL_GROUP = 128
POOL_HIST = 15
D_RNN = 512
RNN_HEADS = 8
CONV_WIDTH = 4
RGLRU_C = 8.0
D_MIX = 1024
D_IN = 1536
N_MEM = 256
MEM_HEADS = 4
MEM_HEAD_DIM = 256
N_EXPERTS = 8
D_FF_CHUNK = 1408
EPS = 1e-6

SUBLANES = 8
LANES = 128
POOL_PAD = 16
CONV_PAD = 8
VMEM_LIMIT = 56 * 1024 * 1024


def _cparams(sem):
    return pltpu.CompilerParams(dimension_semantics=sem, vmem_limit_bytes=VMEM_LIMIT)


def _rms(x, g):
    ms = jnp.mean(x * x, axis=-1, keepdims=True)
    return x * lax.rsqrt(ms + EPS) * g


def _gelu_tanh(x):
    c = math.sqrt(2.0 / math.pi)
    return 0.5 * x * (1.0 + jnp.tanh(c * (x + 0.044715 * (x * x * x))))


def _softplus(x):
    return jnp.maximum(x, 0.0) + jnp.log1p(jnp.exp(-jnp.abs(x)))


def _norm_matmul_body(x_ref, g_ref, w_ref, o_ref):
    hn = _rms(x_ref[...], g_ref[...]).astype(BF16)
    o_ref[...] = jnp.dot(hn, w_ref[...], preferred_element_type=F32)


def _norm_matmul(x, g, w, tm):
    n, d = x.shape
    m = w.shape[1]
    return pl.pallas_call(
        _norm_matmul_body,
        grid=(n // tm,),
        in_specs=[pl.BlockSpec((tm, d), lambda i: (i, 0)),
                  pl.BlockSpec((1, d), lambda i: (0, 0)),
                  pl.BlockSpec((d, m), lambda i: (0, 0))],
        out_specs=pl.BlockSpec((tm, m), lambda i: (i, 0)),
        out_shape=jax.ShapeDtypeStruct((n, m), F32),
        compiler_params=_cparams(("parallel",)),
        name="norm_matmul",
    )(x, g, w)


def _matmul_res_body(a_ref, w_ref, r_ref, o_ref):
    o_ref[...] = r_ref[...] + jnp.dot(a_ref[...].astype(BF16), w_ref[...],
                                      preferred_element_type=F32)


def _matmul_res(a, w, res, tm):
    n, k = a.shape
    m = w.shape[1]
    return pl.pallas_call(
        _matmul_res_body,
        grid=(n // tm,),
        in_specs=[pl.BlockSpec((tm, k), lambda i: (i, 0)),
                  pl.BlockSpec((k, m), lambda i: (0, 0)),
                  pl.BlockSpec((tm, m), lambda i: (i, 0))],
        out_specs=pl.BlockSpec((tm, m), lambda i: (i, 0)),
        out_shape=jax.ShapeDtypeStruct((n, m), F32),
        compiler_params=_cparams(("parallel",)),
        name="matmul_res",
    )(a, w, res)


def _pool_project(z_groups, pw_ref, scale):
    ys = [jnp.dot(z.astype(BF16), pw_ref[g], preferred_element_type=F32)
          for g, z in enumerate(z_groups)]
    return jnp.concatenate(ys, axis=-1) * scale


def _rglru_terms(c, first_pos, wa_ref, ba_ref, wx_ref, bx_ref, lam_ref):
    cb = c.astype(BF16)
    r = jax.nn.sigmoid(jnp.dot(cb, wa_ref[...], preferred_element_type=F32) + ba_ref[...])
    i = jax.nn.sigmoid(jnp.dot(cb, wx_ref[...], preferred_element_type=F32) + bx_ref[...])
    log_a = (-RGLRU_C) * r * _softplus(-lam_ref[...])
    a = jnp.exp(log_a)
    mult = jnp.sqrt(1.0 - a * a)
    if first_pos is not None:
        mult = jnp.where(first_pos, 1.0, mult)
    return a, mult * (i * c)


def _mix_out(y_pool, h, u_gate, gp_ref, gr_ref):
    y_rnn = h * _gelu_tanh(u_gate)
    return jnp.concatenate([_rms(y_pool, gp_ref[...]), _rms(y_rnn, gr_ref[...])], axis=-1)


def _mixer_body(pos0, tt, proj_ref, hist_ref, conv_ref, h0_ref, pw_ref, ps_ref, cw_ref, cb_ref,
                wa_ref, ba_ref, wx_ref, bx_ref, lam_ref, gp_ref, gr_ref,
                mix_ref, hl_ref, pbuf, cbuf, a_s, b_s, hc):
    t = pl.program_id(1)

    @pl.when(t == 0)
    def _():
        pbuf[0:POOL_PAD, :] = hist_ref[...]
        cbuf[0:CONV_PAD, :] = conv_ref[...]
        hc[...] = jnp.broadcast_to(h0_ref[...], (SUBLANES, D_RNN))

    @pl.when(t > 0)
    def _():
        pbuf[0:POOL_PAD, :] = pbuf[tt:tt + POOL_PAD, :]
        cbuf[0:CONV_PAD, :] = cbuf[tt:tt + CONV_PAD, :]

    u_pool = proj_ref[:, 0:D_POOL]
    pbuf[POOL_PAD:POOL_PAD + tt, :] = u_pool
    cbuf[CONV_PAD:CONV_PAD + tt, :] = proj_ref[:, D_POOL:D_MIX]

    pos = pos0 + t * tt + lax.broadcasted_iota(jnp.int32, (tt, 1), 0)

    zs = []
    for g, w in enumerate(POOL_WINDOWS):
        sl = slice(g * POOL_GROUP, (g + 1) * POOL_GROUP)
        acc = pbuf[POOL_PAD:POOL_PAD + tt, sl]
        for k in range(1, w):
            acc = acc + pbuf[POOL_PAD - k:POOL_PAD - k + tt, sl]
        cnt = jnp.minimum(pos + 1, w).astype(F32)
        zs.append(acc / cnt - u_pool[:, sl])
    y_pool = _pool_project(zs, pw_ref, ps_ref[...])

    c = cb_ref[...]
    for k in range(CONV_WIDTH):
        off = CONV_PAD - (CONV_WIDTH - 1) + k
        c = c + cbuf[off:off + tt, :] * cw_ref[k:k + 1, :]

    a, b = _rglru_terms(c, pos == 0, wa_ref, ba_ref, wx_ref, bx_ref, lam_ref)
    a_s[...] = a
    b_s[...] = b

    row = lax.broadcasted_iota(jnp.int32, (SUBLANES, D_RNN), 0)

    def group(j, carry):
        r0 = pl.multiple_of(j * SUBLANES, SUBLANES)
        ga = a_s[pl.ds(r0, SUBLANES), :]
        gb = b_s[pl.ds(r0, SUBLANES), :]
        for s in (1, 2, 4):
            keep = row >= s
            gb = jnp.where(keep, ga * pltpu.roll(gb, s, axis=0) + gb, gb)
            ga = jnp.where(keep, ga * pltpu.roll(ga, s, axis=0), ga)
        h = ga * carry + gb
        b_s[pl.ds(r0, SUBLANES), :] = h
        return jnp.broadcast_to(h[SUBLANES - 1:SUBLANES, :], (SUBLANES, D_RNN))

    carry = lax.fori_loop(0, tt // SUBLANES, group, hc[...])
    hc[...] = carry
    hl_ref[...] = carry

    mix = _mix_out(y_pool, b_s[...], proj_ref[:, D_MIX:D_IN], gp_ref, gr_ref)
    mix_ref[...] = mix.astype(BF16)


def _mixer_prompt(proj, hist, conv, h0, lw, pos0, tt):
    bsz, t_len, _ = proj.shape
    full = lambda shape: pl.BlockSpec(shape, lambda b, t: (0,) * len(shape))
    per_b = lambda rows, cols: pl.BlockSpec((None, rows, cols), lambda b, t: (b, 0, 0))
    return pl.pallas_call(
        functools.partial(_mixer_body, pos0, tt),
        grid=(bsz, t_len // tt),
        in_specs=[pl.BlockSpec((None, tt, D_IN), lambda b, t: (b, t, 0)),
                  per_b(POOL_PAD, D_POOL), per_b(CONV_PAD, D_RNN), per_b(1, D_RNN),
                  full((len(POOL_WINDOWS), POOL_GROUP, POOL_GROUP)), full((1, D_POOL)),
                  full((CONV_WIDTH, D_RNN)), full((1, D_RNN)),
                  full((D_RNN, D_RNN)), full((1, D_RNN)),
                  full((D_RNN, D_RNN)), full((1, D_RNN)),
                  full((1, D_RNN)), full((1, D_POOL)), full((1, D_RNN))],
        out_specs=[pl.BlockSpec((None, tt, D_MIX), lambda b, t: (b, t, 0)),
                   per_b(SUBLANES, D_RNN)],
        out_shape=[jax.ShapeDtypeStruct((bsz, t_len, D_MIX), BF16),
                   jax.ShapeDtypeStruct((bsz, SUBLANES, D_RNN), F32)],
        scratch_shapes=[pltpu.VMEM((POOL_PAD + tt, D_POOL), F32),
                        pltpu.VMEM((CONV_PAD + tt, D_RNN), F32),
                        pltpu.VMEM((tt, D_RNN), F32),
                        pltpu.VMEM((tt, D_RNN), F32),
                        pltpu.VMEM((SUBLANES, D_RNN), F32)],
        compiler_params=_cparams(("parallel", "arbitrary")),
        name="mixer_prompt",
    )(proj, hist, conv, h0, lw["pool_w"], lw["pool_scale"], lw["conv_w"], lw["conv_b"],
      lw["gate_a_w"], lw["gate_a_b"], lw["gate_x_w"], lw["gate_x_b"], lw["lam"],
      lw["out_norm_pool"], lw["out_norm_rnn"])


def _mixer_decode_body(pos0, proj_ref, hist_ref, conv_ref, h0_ref, pw_ref, ps_ref, cw_ref, cb_ref,
                       wa_ref, ba_ref, wx_ref, bx_ref, lam_ref, gp_ref, gr_ref, mix_ref, hn_ref):
    u_pool = proj_ref[:, 0:D_POOL]
    zs = []
    for g, w in enumerate(POOL_WINDOWS):
        sl = slice(g * POOL_GROUP, (g + 1) * POOL_GROUP)
        acc = u_pool[:, sl]
        for k in range(1, w):
            acc = acc + hist_ref[POOL_HIST - k, :, sl]
        zs.append(acc / float(min(pos0 + 1, w)) - u_pool[:, sl])
    y_pool = _pool_project(zs, pw_ref, ps_ref[...])

    c = cb_ref[...]
    for k in range(CONV_WIDTH - 1):
        c = c + conv_ref[k] * cw_ref[k:k + 1, :]
    c = c + proj_ref[:, D_POOL:D_MIX] * cw_ref[CONV_WIDTH - 1:CONV_WIDTH, :]

    a, b = _rglru_terms(c, None, wa_ref, ba_ref, wx_ref, bx_ref, lam_ref)
    h = a * h0_ref[...] + b
    hn_ref[...] = h
    mix_ref[...] = _mix_out(y_pool, h, proj_ref[:, D_MIX:D_IN], gp_ref, gr_ref).astype(BF16)


def _mixer_decode(proj, hist_t, conv_t, h0, lw, pos0):
    assert pos0 > 0
    n = proj.shape[0]
    vm = pl.BlockSpec(memory_space=pltpu.VMEM)
    return pl.pallas_call(
        functools.partial(_mixer_decode_body, pos0),
        in_specs=[vm] * 15,
        out_specs=[vm, vm],
        out_shape=[jax.ShapeDtypeStruct((n, D_MIX), BF16),
                   jax.ShapeDtypeStruct((n, D_RNN), F32)],
        compiler_params=pltpu.CompilerParams(vmem_limit_bytes=VMEM_LIMIT),
        name="mixer_decode",
    )(proj, hist_t, conv_t, h0, lw["pool_w"], lw["pool_scale"], lw["conv_w"], lw["conv_b"],
      lw["gate_a_w"], lw["gate_a_b"], lw["gate_x_w"], lw["gate_x_b"], lw["lam"],
      lw["out_norm_pool"], lw["out_norm_rnn"])


def _softmax_rows(s):
    e = jnp.exp(s - jnp.max(s, axis=-1, keepdims=True))
    return e / jnp.sum(e, axis=-1, keepdims=True)


_NT = (((1,), (1,)), ((), ()))


def _attn_prompt_body(x_ref, k_ref, v_ref, g_ref, wq_ref, wo_ref, o_ref):
    x = x_ref[...]
    xn = _rms(x, g_ref[...]).astype(BF16)
    q = jnp.dot(xn, wq_ref[...], preferred_element_type=F32)
    outs = []
    for h in range(MEM_HEADS):
        sl = slice(h * MEM_HEAD_DIM, (h + 1) * MEM_HEAD_DIM)
        s = lax.dot_general(q[:, sl].astype(BF16), k_ref[:, sl].astype(BF16), _NT,
                            preferred_element_type=F32) * (MEM_HEAD_DIM ** -0.5)
        p = _softmax_rows(s).astype(BF16)
        outs.append(jnp.dot(p, v_ref[:, sl].astype(BF16), preferred_element_type=F32))
    o = jnp.concatenate(outs, axis=-1).astype(BF16)
    o_ref[...] = x + jnp.dot(o, wo_ref[...], preferred_element_type=F32)


def _attn_prompt(x, k, v, g, wq, wo, tt):
    bsz, t_len, d = x.shape
    full = lambda shape: pl.BlockSpec(shape, lambda b, t: (0,) * len(shape))
    kv = pl.BlockSpec((None, N_MEM, d), lambda b, t: (b, 0, 0))
    xs = pl.BlockSpec((None, tt, d), lambda b, t: (b, t, 0))
    return pl.pallas_call(
        _attn_prompt_body,
        grid=(bsz, t_len // tt),
        in_specs=[xs, kv, kv, full((1, d)), full((d, d)), full((d, d))],
        out_specs=xs,
        out_shape=jax.ShapeDtypeStruct(x.shape, F32),
        compiler_params=_cparams(("parallel", "parallel")),
        name="attn_prompt",
    )(x, k, v, g, wq, wo)


def _attn_decode_body(bb, q_ref, k_ref, v_ref, o_ref):
    for b in range(bb):
        q = q_ref[b]
        s = jnp.sum(k_ref[b] * q[None], axis=-1, keepdims=True) * (MEM_HEAD_DIM ** -0.5)
        e = jnp.exp(s - jnp.max(s, axis=0, keepdims=True))
        p = e / jnp.sum(e, axis=0, keepdims=True)
        o_ref[b] = jnp.sum(p * v_ref[b], axis=0)


def _attn_decode(q, k, v, layer, bb):
    n, nh, hd = q.shape
    kv = pl.BlockSpec((None, bb, N_MEM, nh, hd), lambda i: (layer, i, 0, 0, 0))
    qs = pl.BlockSpec((bb, nh, hd), lambda i: (i, 0, 0))
    return pl.pallas_call(
        functools.partial(_attn_decode_body, bb),
        grid=(n // bb,),
        in_specs=[qs, kv, kv],
        out_specs=qs,
        out_shape=jax.ShapeDtypeStruct((n, nh, hd), F32),
        compiler_params=_cparams(("parallel",)),
        name="attn_decode",
    )(q, k, v)


def _router_body(x_ref, g_ref, rw_ref, cw_ref):
    hn = _rms(x_ref[...], g_ref[...])
    logits = jnp.dot(hn, rw_ref[...], precision=lax.Precision.HIGHEST,
                     preferred_element_type=F32)
    lane = lax.broadcasted_iota(jnp.int32, logits.shape, 1)
    neg = -jnp.inf
    lg = jnp.where(lane < N_EXPERTS, logits, neg)
    m1 = jnp.max(lg, axis=-1, keepdims=True)
    i1 = jnp.min(jnp.where(lg == m1, lane, LANES), axis=-1, keepdims=True)
    lg2 = jnp.where(lane == i1, neg, lg)
    m2 = jnp.max(lg2, axis=-1, keepdims=True)
    i2 = jnp.min(jnp.where(lg2 == m2, lane, LANES), axis=-1, keepdims=True)
    e2 = jnp.exp(m2 - m1)
    den = 1.0 + e2
    cw_ref[...] = jnp.where(lane == i1, 1.0 / den, 0.0) + jnp.where(lane == i2, e2 / den, 0.0)


def _router(x, g, rw, tm):
    n, d = x.shape
    return pl.pallas_call(
        _router_body,
        grid=(n // tm,),
        in_specs=[pl.BlockSpec((tm, d), lambda i: (i, 0)),
                  pl.BlockSpec((1, d), lambda i: (0, 0)),
                  pl.BlockSpec((d, LANES), lambda i: (0, 0))],
        out_specs=pl.BlockSpec((tm, LANES), lambda i: (i, 0)),
        out_shape=jax.ShapeDtypeStruct((n, LANES), F32),
        compiler_params=_cparams(("parallel",)),
        name="router",
    )(x, g, rw)


def _ffn_body(n_chunks, routed, final, *refs):
    x_ref, g_ref, wg_ref, wu_ref, wd_ref = refs[:5]
    rest = list(refs[5:])
    cw_ref = rest.pop(0) if routed else None
    fg_ref = rest.pop(0) if final else None
    o_ref, hn_s = rest
    j = pl.program_id(1)

    @pl.when(j == 0)
    def _():
        hn_s[...] = _rms(x_ref[...], g_ref[...]).astype(BF16)
        o_ref[...] = x_ref[...]

    hn = hn_s[...]
    gate = jnp.dot(hn, wg_ref[...], preferred_element_type=F32)
    up = jnp.dot(hn, wu_ref[...], preferred_element_type=F32)
    mid = (gate * jax.nn.sigmoid(gate) * up).astype(BF16)
    y = jnp.dot(mid, wd_ref[...], preferred_element_type=F32)
    if routed:
        cw = cw_ref[...]
        lane = lax.broadcasted_iota(jnp.int32, cw.shape, 1)
        y = y * jnp.sum(jnp.where(lane == j, cw, 0.0), axis=-1, keepdims=True)
    o_ref[...] += y

    if final:
        @pl.when(j == n_chunks - 1)
        def _():
            o_ref[...] = _rms(o_ref[...], fg_ref[...])


def _ffn(x, g, wg, wu, wd, tm, cw=None, final_g=None):
    n, d = x.shape
    routed = cw is not None
    final = final_g is not None
    fc = D_FF_CHUNK
    if routed:
        n_chunks = wg.shape[0]
        w_in_spec = pl.BlockSpec((None, d, fc), lambda i, j: (j, 0, 0))
        w_dn_spec = pl.BlockSpec((None, fc, d), lambda i, j: (j, 0, 0))
    else:
        n_chunks = wg.shape[1] // fc
        w_in_spec = pl.BlockSpec((d, fc), lambda i, j: (0, j))
        w_dn_spec = pl.BlockSpec((fc, d), lambda i, j: (j, 0))
    row = pl.BlockSpec((tm, d), lambda i, j: (i, 0))
    vec = pl.BlockSpec((1, d), lambda i, j: (0, 0))
    in_specs = [row, vec, w_in_spec, w_in_spec, w_dn_spec]
    args = [x, g, wg, wu, wd]
    if routed:
        in_specs.append(pl.BlockSpec((tm, LANES), lambda i, j: (i, 0)))
        args.append(cw)
    if final:
        in_specs.append(vec)
        args.append(final_g)
    return pl.pallas_call(
        functools.partial(_ffn_body, n_chunks, routed, final),
        grid=(n // tm, n_chunks),
        in_specs=in_specs,
        out_specs=row,
        out_shape=jax.ShapeDtypeStruct((n, d), F32),
        scratch_shapes=[pltpu.VMEM((tm, d), BF16)],
        compiler_params=_cparams(("parallel", "arbitrary")),
        name="ffn",
    )(*args)


def _block_diag(w):
    h, a, b = w.shape
    eye = jnp.eye(h, dtype=w.dtype)
    return (eye[:, None, :, None] * w[:, :, None, :]).reshape(h * a, h * b)


def _layer_weights(l, P):
    row = lambda v: v[l].reshape(1, -1)
    return dict(
        norm_mix=row(P["norm_mix"]), w_in=P["w_in"][l].astype(BF16),
        pool_w=P["pool_w"][l].astype(BF16), pool_scale=row(P["pool_scale"]),
        conv_w=P["conv_w"][l], conv_b=row(P["conv_b"]),
        gate_a_w=_block_diag(P["gate_a_w"][l]).astype(BF16), gate_a_b=row(P["gate_a_b"]),
        gate_x_w=_block_diag(P["gate_x_w"][l]).astype(BF16), gate_x_b=row(P["gate_x_b"]),
        lam=row(P["rglru_lambda"]),
        out_norm_pool=row(P["out_norm_pool"]), out_norm_rnn=row(P["out_norm_rnn"]),
        w_out=P["w_out"][l].astype(BF16), norm_mem=row(P["norm_mem"]),
        w_mq=P["w_mq"][l].astype(BF16), w_mo=P["w_mo"][l].astype(BF16),
        norm_ffn=row(P["norm_ffn"]))


def _channel_mixer(x, l, lw, P, tm, final_g):
    j = l // 2
    if l % 2 == 0:
        return _ffn(x, lw["norm_ffn"], P["ffn_w_gate"][j].astype(BF16),
                    P["ffn_w_up"][j].astype(BF16), P["ffn_w_down"][j].astype(BF16),
                    tm, final_g=final_g)
    rw = jnp.pad(P["router_w"][j], ((0, 0), (0, LANES - N_EXPERTS)))
    cw = _router(x, lw["norm_ffn"], rw, tm)
    return _ffn(x, lw["norm_ffn"], P["moe_w_gate"][j].astype(BF16),
                P["moe_w_up"][j].astype(BF16), P["moe_w_down"][j].astype(BF16),
                tm, cw=cw, final_g=final_g)


def kernel(x_prompt, x_sample, mem_prompt, state_pool, state_conv, state_h, cache_mem_k, cache_mem_v,
           norm_mix, w_in, pool_w, pool_scale, conv_w, conv_b, gate_a_w, gate_a_b, gate_x_w, gate_x_b,
           rglru_lambda, out_norm_pool, out_norm_rnn, w_out, norm_mem, mem_norm, w_mq, w_mk, w_mv, w_mo,
           norm_ffn, ffn_w_gate, ffn_w_up, ffn_w_down, router_w, moe_w_gate, moe_w_up, moe_w_down,
           final_norm):
    P = dict(norm_mix=norm_mix, w_in=w_in, pool_w=pool_w, pool_scale=pool_scale, conv_w=conv_w,
             conv_b=conv_b, gate_a_w=gate_a_w, gate_a_b=gate_a_b, gate_x_w=gate_x_w,
             gate_x_b=gate_x_b, rglru_lambda=rglru_lambda, out_norm_pool=out_norm_pool,
             out_norm_rnn=out_norm_rnn, w_out=w_out, norm_mem=norm_mem, w_mq=w_mq, w_mo=w_mo,
             norm_ffn=norm_ffn, ffn_w_gate=ffn_w_gate, ffn_w_up=ffn_w_up, ffn_w_down=ffn_w_down,
             router_w=router_w, moe_w_gate=moe_w_gate, moe_w_up=moe_w_up, moe_w_down=moe_w_down)
    bsz, t_len, d = x_prompt.shape
    n_dec = x_sample.shape[0]
    n_tok = bsz * t_len
    fg = final_norm.reshape(1, d)
    TM, TT, TM_DEC, BB = 512, 512, n_dec, SUBLANES

    xp = x_prompt
    xs = x_sample.reshape(n_dec, d)
    mem2 = mem_prompt.reshape(bsz * N_MEM, d)
    mem_ks, mem_vs, pools_p, convs_p, hs_p, pools_s, convs_s, hs_s = ([] for _ in range(8))
    zeros_hist = jnp.zeros((bsz, POOL_PAD, D_POOL), F32)
    zeros_conv = jnp.zeros((bsz, CONV_PAD, D_RNN), F32)
    zeros_h = jnp.zeros((bsz, 1, D_RNN), F32)

    for l in range(DEPTH):
        lw = _layer_weights(l, P)
        last_g = fg if l == DEPTH - 1 else None

        mg = mem_norm[l].reshape(1, d)
        mk = _norm_matmul(mem2, mg, w_mk[l].astype(BF16), TM).reshape(bsz, N_MEM, d)
        mv = _norm_matmul(mem2, mg, w_mv[l].astype(BF16), TM).reshape(bsz, N_MEM, d)
        mem_ks.append(mk.reshape(bsz, N_MEM, MEM_HEADS, MEM_HEAD_DIM))
        mem_vs.append(mv.reshape(bsz, N_MEM, MEM_HEADS, MEM_HEAD_DIM))

        proj = _norm_matmul(xp.reshape(n_tok, d), lw["norm_mix"], lw["w_in"], TM)
        proj = proj.reshape(bsz, t_len, D_IN)
        mixed, h_last = _mixer_prompt(proj, zeros_hist, zeros_conv, zeros_h, lw, 0, TT)
        pools_p.append(proj[:, t_len - POOL_HIST:, :D_POOL])
        convs_p.append(proj[:, t_len - (CONV_WIDTH - 1):, D_POOL:D_MIX])
        hs_p.append(h_last[:, 0, :])
        x2 = _matmul_res(mixed.reshape(n_tok, D_MIX), lw["w_out"], xp.reshape(n_tok, d), TM)
        x3 = _attn_prompt(x2.reshape(bsz, t_len, d), mk, mv, lw["norm_mem"], lw["w_mq"],
                          lw["w_mo"], TT)
        xp = _channel_mixer(x3.reshape(n_tok, d), l, lw, P, TM, last_g).reshape(bsz, t_len, d)

        proj_s = _norm_matmul(xs, lw["norm_mix"], lw["w_in"], TM_DEC)
        mixed_s, h_new = _mixer_decode(proj_s, jnp.swapaxes(state_pool[l], 0, 1),
                                       jnp.swapaxes(state_conv[l], 0, 1), state_h[l], lw,
                                       PAST_LEN)
        pools_s.append(jnp.concatenate([state_pool[l][:, 1:], proj_s[:, None, :D_POOL]], axis=1))
        convs_s.append(jnp.concatenate([state_conv[l][:, 1:], proj_s[:, None, D_POOL:D_MIX]],
                                       axis=1))
        hs_s.append(h_new)
        xs2 = _matmul_res(mixed_s, lw["w_out"], xs, TM_DEC)
        q_s = _norm_matmul(xs2, lw["norm_mem"], lw["w_mq"], TM_DEC)
        att = _attn_decode(q_s.reshape(n_dec, MEM_HEADS, MEM_HEAD_DIM), cache_mem_k, cache_mem_v,
                           l, BB)
        xs3 = _matmul_res(att.reshape(n_dec, d), lw["w_mo"], xs2, TM_DEC)
        xs = _channel_mixer(xs3, l, lw, P, TM_DEC, last_g)

    return (xp, xs.reshape(n_dec, 1, d), jnp.stack(pools_p), jnp.stack(convs_p), jnp.stack(hs_p),
            jnp.stack(mem_ks), jnp.stack(mem_vs), jnp.stack(pools_s), jnp.stack(convs_s),
            jnp.stack(hs_s))
```

```python
import functools
import math

import jax
import jax.numpy as jnp
from jax import lax
from jax.experimental import pallas as pl
from jax.experimental.pallas import tpu as pltpu

F32 = jnp.float32
BF16 = jnp.bfloat16

D_MODEL = 1024
DEPTH = 2
PAST_LEN = 16384
D_POOL = 512
POOL_WINDOWS = (2, 4, 8, 16)
POOL_GROUP = 128
POOL_HIST = 15
D_RNN = 512
RNN_HEADS = 8
CONV_WIDTH = 4
RGLRU_C = 8.0
D_MIX = 1024
D_IN = 1536
N_MEM = 256
MEM_HEADS = 4
MEM_HEAD_DIM = 256
N_EXPERTS = 8
D_FF_CHUNK = 1408
EPS = 1e-6

SUBLANES = 8
LANES = 128
POOL_PAD = 16
CONV_PAD = 8
VMEM_LIMIT = 56 * 1024 * 1024
SLOT_MAP_CHUNK = 2048
TOK_BITS = 14
MOE_CHUNK = 256
MOE_TILE = 576


def _cparams(sem):
    return pltpu.CompilerParams(dimension_semantics=sem, vmem_limit_bytes=VMEM_LIMIT)


def _rms(x, g):
    ms = jnp.mean(x * x, axis=-1, keepdims=True)
    return x * lax.rsqrt(ms + EPS) * g


def _gelu_tanh(x):
    c = math.sqrt(2.0 / math.pi)
    return 0.5 * x * (1.0 + jnp.tanh(c * (x + 0.044715 * (x * x * x))))


def _softplus(x):
    return jnp.maximum(x, 0.0) + jnp.log1p(jnp.exp(-jnp.abs(x)))


def _norm_matmul_body(x_ref, g_ref, w_ref, o_ref):
    hn = _rms(x_ref[...], g_ref[...]).astype(BF16)
    o_ref[...] = jnp.dot(hn, w_ref[...], preferred_element_type=F32)


def _norm_matmul(x, g, w, tm):
    n, d = x.shape
    m = w.shape[1]
    return pl.pallas_call(
        _norm_matmul_body,
        grid=(n // tm,),
        in_specs=[pl.BlockSpec((tm, d), lambda i: (i, 0)),
                  pl.BlockSpec((1, d), lambda i: (0, 0)),
                  pl.BlockSpec((d, m), lambda i: (0, 0))],
        out_specs=pl.BlockSpec((tm, m), lambda i: (i, 0)),
        out_shape=jax.ShapeDtypeStruct((n, m), F32),
        compiler_params=_cparams(("parallel",)),
        name="norm_matmul",
    )(x, g, w)


def _matmul_res_body(a_ref, w_ref, r_ref, o_ref):
    o_ref[...] = r_ref[...] + jnp.dot(a_ref[...].astype(BF16), w_ref[...],
                                      preferred_element_type=F32)


def _matmul_res(a, w, res, tm):
    n, k = a.shape
    m = w.shape[1]
    return pl.pallas_call(
        _matmul_res_body,
        grid=(n // tm,),
        in_specs=[pl.BlockSpec((tm, k), lambda i: (i, 0)),
                  pl.BlockSpec((k, m), lambda i: (0, 0)),
                  pl.BlockSpec((tm, m), lambda i: (i, 0))],
        out_specs=pl.BlockSpec((tm, m), lambda i: (i, 0)),
        out_shape=jax.ShapeDtypeStruct((n, m), F32),
        compiler_params=_cparams(("parallel",)),
        name="matmul_res",
    )(a, w, res)


def _pool_project(z_groups, pw_ref, scale):
    ys = [jnp.dot(z.astype(BF16), pw_ref[g], preferred_element_type=F32)
          for g, z in enumerate(z_groups)]
    return jnp.concatenate(ys, axis=-1) * scale


def _rglru_terms(c, first_pos, wa_ref, ba_ref, wx_ref, bx_ref, lam_ref):
    cb = c.astype(BF16)
    r = jax.nn.sigmoid(jnp.dot(cb, wa_ref[...], preferred_element_type=F32) + ba_ref[...])
    i = jax.nn.sigmoid(jnp.dot(cb, wx_ref[...], preferred_element_type=F32) + bx_ref[...])
    log_a = (-RGLRU_C) * r * _softplus(-lam_ref[...])
    a = jnp.exp(log_a)
    mult = jnp.sqrt(1.0 - a * a)
    if first_pos is not None:
        mult = jnp.where(first_pos, 1.0, mult)
    return a, mult * (i * c)


def _mix_out(y_pool, h, u_gate, gp_ref, gr_ref):
    y_rnn = h * _gelu_tanh(u_gate)
    return jnp.concatenate([_rms(y_pool, gp_ref[...]), _rms(y_rnn, gr_ref[...])], axis=-1)


def _mixer_body(pos0, tt, proj_ref, hist_ref, conv_ref, h0_ref, pw_ref, ps_ref, cw_ref, cb_ref,
                wa_ref, ba_ref, wx_ref, bx_ref, lam_ref, gp_ref, gr_ref,
                mix_ref, hl_ref, pbuf, cbuf, a_s, b_s, hc):
    t = pl.program_id(1)

    @pl.when(t == 0)
    def _():
        pbuf[0:POOL_PAD, :] = hist_ref[...]
        cbuf[0:CONV_PAD, :] = conv_ref[...]
        hc[...] = jnp.broadcast_to(h0_ref[...], (SUBLANES, D_RNN))

    @pl.when(t > 0)
    def _():
        pbuf[0:POOL_PAD, :] = pbuf[tt:tt + POOL_PAD, :]
        cbuf[0:CONV_PAD, :] = cbuf[tt:tt + CONV_PAD, :]

    u_pool = proj_ref[:, 0:D_POOL]
    pbuf[POOL_PAD:POOL_PAD + tt, :] = u_pool
    cbuf[CONV_PAD:CONV_PAD + tt, :] = proj_ref[:, D_POOL:D_MIX]

    pos = pos0 + t * tt + lax.broadcasted_iota(jnp.int32, (tt, 1), 0)

    zs = []
    for g, w in enumerate(POOL_WINDOWS):
        sl = slice(g * POOL_GROUP, (g + 1) * POOL_GROUP)
        acc = pbuf[POOL_PAD:POOL_PAD + tt, sl]
        for k in range(1, w):
            acc = acc + pbuf[POOL_PAD - k:POOL_PAD - k + tt, sl]
        cnt = jnp.minimum(pos + 1, w).astype(F32)
        zs.append(acc / cnt - u_pool[:, sl])
    y_pool = _pool_project(zs, pw_ref, ps_ref[...])

    c = cb_ref[...]
    for k in range(CONV_WIDTH):
        off = CONV_PAD - (CONV_WIDTH - 1) + k
        c = c + cbuf[off:off + tt, :] * cw_ref[k:k + 1, :]

    a, b = _rglru_terms(c, pos == 0, wa_ref, ba_ref, wx_ref, bx_ref, lam_ref)
    a_s[...] = a
    b_s[...] = b

    row = lax.broadcasted_iota(jnp.int32, (SUBLANES, D_RNN), 0)

    def group(j, carry):
        r0 = pl.multiple_of(j * SUBLANES, SUBLANES)
        ga = a_s[pl.ds(r0, SUBLANES), :]
        gb = b_s[pl.ds(r0, SUBLANES), :]
        for s in (1, 2, 4):
            keep = row >= s
            gb = jnp.where(keep, ga * pltpu.roll(gb, s, axis=0) + gb, gb)
            ga = jnp.where(keep, ga * pltpu.roll(ga, s, axis=0), ga)
        h = ga * carry + gb
        b_s[pl.ds(r0, SUBLANES), :] = h
        return jnp.broadcast_to(h[SUBLANES - 1:SUBLANES, :], (SUBLANES, D_RNN))

    carry = lax.fori_loop(0, tt // SUBLANES, group, hc[...])
    hc[...] = carry
    hl_ref[...] = carry

    mix = _mix_out(y_pool, b_s[...], proj_ref[:, D_MIX:D_IN], gp_ref, gr_ref)
    mix_ref[...] = mix.astype(BF16)


def _mixer_prompt(proj, hist, conv, h0, lw, pos0, tt):
    bsz, t_len, _ = proj.shape
    full = lambda shape: pl.BlockSpec(shape, lambda b, t: (0,) * len(shape))
    per_b = lambda rows, cols: pl.BlockSpec((None, rows, cols), lambda b, t: (b, 0, 0))
    return pl.pallas_call(
        functools.partial(_mixer_body, pos0, tt),
        grid=(bsz, t_len // tt),
        in_specs=[pl.BlockSpec((None, tt, D_IN), lambda b, t: (b, t, 0)),
                  per_b(POOL_PAD, D_POOL), per_b(CONV_PAD, D_RNN), per_b(1, D_RNN),
                  full((len(POOL_WINDOWS), POOL_GROUP, POOL_GROUP)), full((1, D_POOL)),
                  full((CONV_WIDTH, D_RNN)), full((1, D_RNN)),
                  full((D_RNN, D_RNN)), full((1, D_RNN)),
                  full((D_RNN, D_RNN)), full((1, D_RNN)),
                  full((1, D_RNN)), full((1, D_POOL)), full((1, D_RNN))],
        out_specs=[pl.BlockSpec((None, tt, D_MIX), lambda b, t: (b, t, 0)),
                   per_b(SUBLANES, D_RNN)],
        out_shape=[jax.ShapeDtypeStruct((bsz, t_len, D_MIX), BF16),
                   jax.ShapeDtypeStruct((bsz, SUBLANES, D_RNN), F32)],
        scratch_shapes=[pltpu.VMEM((POOL_PAD + tt, D_POOL), F32),
                        pltpu.VMEM((CONV_PAD + tt, D_RNN), F32),
                        pltpu.VMEM((tt, D_RNN), F32),
                        pltpu.VMEM((tt, D_RNN), F32),
                        pltpu.VMEM((SUBLANES, D_RNN), F32)],
        compiler_params=_cparams(("parallel", "arbitrary")),
        name="mixer_prompt",
    )(proj, hist, conv, h0, lw["pool_w"], lw["pool_scale"], lw["conv_w"], lw["conv_b"],
      lw["gate_a_w"], lw["gate_a_b"], lw["gate_x_w"], lw["gate_x_b"], lw["lam"],
      lw["out_norm_pool"], lw["out_norm_rnn"])


def _mixer_decode_body(pos0, proj_ref, hist_ref, conv_ref, h0_ref, pw_ref, ps_ref, cw_ref, cb_ref,
                       wa_ref, ba_ref, wx_ref, bx_ref, lam_ref, gp_ref, gr_ref, mix_ref, hn_ref):
    u_pool = proj_ref[:, 0:D_POOL]
    zs = []
    for g, w in enumerate(POOL_WINDOWS):
        sl = slice(g * POOL_GROUP, (g + 1) * POOL_GROUP)
        acc = u_pool[:, sl]
        for k in range(1, w):
            acc = acc + hist_ref[POOL_HIST - k, :, sl]
        zs.append(acc / float(min(pos0 + 1, w)) - u_pool[:, sl])
    y_pool = _pool_project(zs, pw_ref, ps_ref[...])

    c = cb_ref[...]
    for k in range(CONV_WIDTH - 1):
        c = c + conv_ref[k] * cw_ref[k:k + 1, :]
    c = c + proj_ref[:, D_POOL:D_MIX] * cw_ref[CONV_WIDTH - 1:CONV_WIDTH, :]

    a, b = _rglru_terms(c, None, wa_ref, ba_ref, wx_ref, bx_ref, lam_ref)
    h = a * h0_ref[...] + b
    hn_ref[...] = h
    mix_ref[...] = _mix_out(y_pool, h, proj_ref[:, D_MIX:D_IN], gp_ref, gr_ref).astype(BF16)


def _mixer_decode(proj, hist_t, conv_t, h0, lw, pos0):
    assert pos0 > 0
    n = proj.shape[0]
    vm = pl.BlockSpec(memory_space=pltpu.VMEM)
    return pl.pallas_call(
        functools.partial(_mixer_decode_body, pos0),
        in_specs=[vm] * 15,
        out_specs=[vm, vm],
        out_shape=[jax.ShapeDtypeStruct((n, D_MIX), BF16),
                   jax.ShapeDtypeStruct((n, D_RNN), F32)],
        compiler_params=pltpu.CompilerParams(vmem_limit_bytes=VMEM_LIMIT),
        name="mixer_decode",
    )(proj, hist_t, conv_t, h0, lw["pool_w"], lw["pool_scale"], lw["conv_w"], lw["conv_b"],
      lw["gate_a_w"], lw["gate_a_b"], lw["gate_x_w"], lw["gate_x_b"], lw["lam"],
      lw["out_norm_pool"], lw["out_norm_rnn"])


def _softmax_rows(s):
    e = jnp.exp(s - jnp.max(s, axis=-1, keepdims=True))
    return e / jnp.sum(e, axis=-1, keepdims=True)


_NT = (((1,), (1,)), ((), ()))


def _attn_prompt_body(x_ref, k_ref, v_ref, g_ref, wq_ref, wo_ref, o_ref):
    x = x_ref[...]
    xn = _rms(x, g_ref[...]).astype(BF16)
    q = jnp.dot(xn, wq_ref[...], preferred_element_type=F32)
    outs = []
    for h in range(MEM_HEADS):
        sl = slice(h * MEM_HEAD_DIM, (h + 1) * MEM_HEAD_DIM)
        s = lax.dot_general(q[:, sl].astype(BF16), k_ref[:, sl].astype(BF16), _NT,
                            preferred_element_type=F32) * (MEM_HEAD_DIM ** -0.5)
        p = _softmax_rows(s).astype(BF16)
        outs.append(jnp.dot(p, v_ref[:, sl].astype(BF16), preferred_element_type=F32))
    o = jnp.concatenate(outs, axis=-1).astype(BF16)
    o_ref[...] = x + jnp.dot(o, wo_ref[...], preferred_element_type=F32)


def _attn_prompt(x, k, v, g, wq, wo, tt):
    bsz, t_len, d = x.shape
    full = lambda shape: pl.BlockSpec(shape, lambda b, t: (0,) * len(shape))
    kv = pl.BlockSpec((None, N_MEM, d), lambda b, t: (b, 0, 0))
    xs = pl.BlockSpec((None, tt, d), lambda b, t: (b, t, 0))
    return pl.pallas_call(
        _attn_prompt_body,
        grid=(bsz, t_len // tt),
        in_specs=[xs, kv, kv, full((1, d)), full((d, d)), full((d, d))],
        out_specs=xs,
        out_shape=jax.ShapeDtypeStruct(x.shape, F32),
        compiler_params=_cparams(("parallel", "parallel")),
        name="attn_prompt",
    )(x, k, v, g, wq, wo)


def _attn_decode_body(bb, q_ref, k_ref, v_ref, o_ref):
    for b in range(bb):
        q = q_ref[b]
        s = jnp.sum(k_ref[b] * q[None], axis=-1, keepdims=True) * (MEM_HEAD_DIM ** -0.5)
        e = jnp.exp(s - jnp.max(s, axis=0, keepdims=True))
        p = e / jnp.sum(e, axis=0, keepdims=True)
        o_ref[b] = jnp.sum(p * v_ref[b], axis=0)


def _attn_decode(q, k, v, layer, bb):
    n, nh, hd = q.shape
    kv = pl.BlockSpec((None, bb, N_MEM, nh, hd), lambda i: (layer, i, 0, 0, 0))
    qs = pl.BlockSpec((bb, nh, hd), lambda i: (i, 0, 0))
    return pl.pallas_call(
        functools.partial(_attn_decode_body, bb),
        grid=(n // bb,),
        in_specs=[qs, kv, kv],
        out_specs=qs,
        out_shape=jax.ShapeDtypeStruct((n, nh, hd), F32),
        compiler_params=_cparams(("parallel",)),
        name="attn_decode",
    )(q, k, v)


def _top2(hn, rw_ref):
    logits = jnp.dot(hn, rw_ref[...], precision=lax.Precision.HIGHEST,
                     preferred_element_type=F32)
    lane = lax.broadcasted_iota(jnp.int32, logits.shape, 1)
    neg = -jnp.inf
    lg = jnp.where(lane < N_EXPERTS, logits, neg)
    m1 = jnp.max(lg, axis=-1, keepdims=True)
    i1 = jnp.min(jnp.where(lg == m1, lane, LANES), axis=-1, keepdims=True)
    lg2 = jnp.where(lane == i1, neg, lg)
    m2 = jnp.max(lg2, axis=-1, keepdims=True)
    i2 = jnp.min(jnp.where(lg2 == m2, lane, LANES), axis=-1, keepdims=True)
    e2 = jnp.exp(m2 - m1)
    den = 1.0 + e2
    return lane, i1, i2, 1.0 / den, e2 / den


def _router_body(x_ref, g_ref, rw_ref, cw_ref):
    lane, i1, i2, g1, g2 = _top2(_rms(x_ref[...], g_ref[...]), rw_ref)
    cw_ref[...] = jnp.where(lane == i1, g1, 0.0) + jnp.where(lane == i2, g2, 0.0)


def _router(x, g, rw, tm):
    n, d = x.shape
    return pl.pallas_call(
        _router_body,
        grid=(n // tm,),
        in_specs=[pl.BlockSpec((tm, d), lambda i: (i, 0)),
                  pl.BlockSpec((1, d), lambda i: (0, 0)),
                  pl.BlockSpec((d, LANES), lambda i: (0, 0))],
        out_specs=pl.BlockSpec((tm, LANES), lambda i: (i, 0)),
        out_shape=jax.ShapeDtypeStruct((n, LANES), F32),
        compiler_params=_cparams(("parallel",)),
        name="router",
    )(x, g, rw)


def _store_row_tiles(ref, val):
    rows = val.shape[0]
    for c in range(val.shape[1] // LANES):
        ref[pl.ds(c, rows, stride=SUBLANES), :] = val[:, c * LANES:(c + 1) * LANES]


def _load_row_tiles(ref, rows):
    return jnp.concatenate([ref[pl.ds(c, rows, stride=SUBLANES), :] for c in range(SUBLANES)],
                           axis=-1)


R_E1, R_E2, R_RANK1, R_RANK2, R_G1, R_G2 = range(6)


def _router_rank_body(x_ref, g_ref, rw_ref, rec_ref, rows_ref, cnt_ref, hn_ref, carry):
    i = pl.program_id(0)
    tm = x_ref.shape[0]

    @pl.when(i == 0)
    def _():
        carry[...] = jnp.zeros_like(carry)

    hn = _rms(x_ref[...], g_ref[...])
    _store_row_tiles(hn_ref, hn)
    lane, i1, i2, g1, g2 = _top2(hn, rw_ref)
    picks = ((lane == i1) | (lane == i2)).astype(F32)
    r_io = lax.broadcasted_iota(jnp.int32, (tm, tm), 0)
    c_io = lax.broadcasted_iota(jnp.int32, (tm, tm), 1)
    earlier = (c_io < r_io).astype(BF16)
    before = jnp.dot(earlier, picks.astype(BF16), preferred_element_type=F32) + carry[...]
    rank1 = jnp.sum(jnp.where(lane == i1, before, 0.0), axis=-1, keepdims=True)
    rank2 = jnp.sum(jnp.where(lane == i2, before, 0.0), axis=-1, keepdims=True)
    carry[...] += jnp.sum(picks, axis=0, keepdims=True)
    cnt_ref[...] = jnp.broadcast_to(carry[...], cnt_ref.shape)

    rec = jnp.zeros((tm, LANES), F32)
    for k, v in ((R_E1, i1.astype(F32)), (R_E2, i2.astype(F32)), (R_RANK1, rank1),
                 (R_RANK2, rank2), (R_G1, g1), (R_G2, g2)):
        rec = jnp.where(lane == k, v, rec)
    rec_ref[...] = rec
    rows_ref[...] = rec.T[0:SUBLANES, :].astype(jnp.int32)


def _router_rank(x, g, rw, tm):
    n, d = x.shape
    assert d == SUBLANES * LANES
    return pl.pallas_call(
        _router_rank_body,
        grid=(n // tm,),
        in_specs=[pl.BlockSpec((tm, d), lambda i: (i, 0)),
                  pl.BlockSpec((1, d), lambda i: (0, 0)),
                  pl.BlockSpec((d, LANES), lambda i: (0, 0))],
        out_specs=[pl.BlockSpec((tm, LANES), lambda i: (i, 0)),
                   pl.BlockSpec((SUBLANES, tm), lambda i: (0, i)),
                   pl.BlockSpec((SUBLANES, LANES), lambda i: (0, 0)),
                   pl.BlockSpec((tm * SUBLANES, LANES), lambda i: (i, 0))],
        out_shape=[jax.ShapeDtypeStruct((n, LANES), F32),
                   jax.ShapeDtypeStruct((SUBLANES, n), jnp.int32),
                   jax.ShapeDtypeStruct((SUBLANES, LANES), F32),
                   jax.ShapeDtypeStruct((n * SUBLANES, LANES), F32)],
        scratch_shapes=[pltpu.VMEM((1, LANES), F32)],
        compiler_params=_cparams(("arbitrary",)),
        name="router_rank",
    )(x, g, rw)


def _slot_map_body(n_tok, tm, n_tiles, cnt_ref, rows_ref, slot_ref, te_ref, nu_ref, off_s):
    c = pl.program_id(0)
    tc = rows_ref.shape[1]

    def fill_pad(s, ordinal):
        slot_ref[s] = (2 * n_tok + ordinal) << TOK_BITS
        return ordinal + 1

    @pl.when(c == 0)
    def _():
        lead = lax.fori_loop(0, tm, fill_pad, jnp.int32(0))

        def per_expert(e, carry):
            tile0, ordinal = carry
            n = cnt_ref[e]
            start = (tile0 + 1) * tm
            off_s[e] = start
            tile1 = tile0 + lax.div(n + (tm - 1), tm)

            def fill_tile(t, z):
                te_ref[t] = e
                return z

            lax.fori_loop(tile0, tile1, fill_tile, 0)
            ordinal = lax.fori_loop(start + n, (tile1 + 1) * tm, fill_pad, ordinal)
            return tile1, ordinal

        used, ordinal = lax.fori_loop(0, N_EXPERTS, per_expert, (jnp.int32(0), lead))
        nu_ref[0] = used

        def fill_rest(t, z):
            te_ref[t] = N_EXPERTS - 1
            return z

        lax.fori_loop(used, n_tiles, fill_rest, 0)
        lax.fori_loop((used + 1) * tm, (n_tiles + 2) * tm, fill_pad, ordinal)

    def per_token(j, z):
        t = c * tc + j
        first = (t << TOK_BITS) | t
        slot_ref[off_s[rows_ref[R_E1, j]] + rows_ref[R_RANK1, j]] = first
        slot_ref[off_s[rows_ref[R_E2, j]] + rows_ref[R_RANK2, j]] = first + (n_tok << TOK_BITS)
        return z

    lax.fori_loop(0, tc, per_token, 0, unroll=8)


def _slot_map(counts, rows, n_tok, tm, n_tiles, tc):
    smem = pl.BlockSpec(memory_space=pltpu.SMEM)
    return pl.pallas_call(
        functools.partial(_slot_map_body, n_tok, tm, n_tiles),
        grid=(n_tok // tc,),
        in_specs=[smem, pl.BlockSpec((SUBLANES, tc), lambda c: (0, c), memory_space=pltpu.SMEM)],
        out_specs=[smem, smem, smem],
        out_shape=[jax.ShapeDtypeStruct(((n_tiles + 2) * tm,), jnp.int32),
                   jax.ShapeDtypeStruct((n_tiles,), jnp.int32),
                   jax.ShapeDtypeStruct((1,), jnp.int32)],
        scratch_shapes=[pltpu.SMEM((N_EXPERTS,), jnp.int32)],
        compiler_params=_cparams(("arbitrary",)),
        name="slot_map",
    )(counts, rows)


def _experts_body(tm, slot_ref, te_ref, nu_ref, x_hbm, wg_ref, wu_ref, wd_ref,
                  out_hbm, xbuf, ybuf, hn_s, mid_s, acc, gsem, ssem, zsem):
    del te_ref
    i = pl.program_id(0)
    n_used = nu_ref[0]
    cur = i % 2
    nxt = 1 - cur

    def tile_rows(r):
        return pl.ds(pl.multiple_of(r * SUBLANES, SUBLANES), SUBLANES)

    def gather_copy(src_row, r, buf):
        return pltpu.make_async_copy(x_hbm.at[tile_rows(src_row), :],
                                     xbuf.at[buf, tile_rows(r), :], gsem.at[buf])

    def scatter_copy(dst_row, r, buf):
        return pltpu.make_async_copy(ybuf.at[buf, tile_rows(r), :],
                                     out_hbm.at[tile_rows(dst_row), :], ssem.at[buf])

    def gather_src(tile, r):
        return slot_ref[(tile + 1) * tm + r] & ((1 << TOK_BITS) - 1)

    def scatter_dst(tile, r):
        return slot_ref[(tile + 1) * tm + r] >> TOK_BITS

    def wait_gathers(buf):
        for r in range(tm):
            gather_copy(0, r, buf).wait()

    def wait_scatters(buf):
        for r in range(tm):
            scatter_copy(0, r, buf).wait()

    @pl.when(i == 0)
    def _():
        ybuf[...] = jnp.zeros_like(ybuf)

        def zero_tile(j, z):
            rows = pl.ds(pl.multiple_of(j * (tm * SUBLANES), SUBLANES), tm * SUBLANES)
            cp = pltpu.make_async_copy(ybuf.at[0], out_hbm.at[rows, :], zsem.at[0])
            cp.start()
            cp.wait()
            return z

        lax.fori_loop(n_used + 1, out_hbm.shape[0] // (tm * SUBLANES), zero_tile, 0)
        for r in range(tm):
            gather_copy(gather_src(0, r), r, 0).start()

    @pl.when(i < n_used)
    def _():
        wait_gathers(cur)
        hn_s[...] = _load_row_tiles(xbuf.at[cur], tm).astype(BF16)
        acc[...] = jnp.zeros_like(acc)
        mid_s[...] = jnp.zeros_like(mid_s)
        n_chunks = wg_ref.shape[0]
        rows_per = tm // n_chunks

        def chunk(k, carry):
            row0 = pl.multiple_of(k * rows_per, SUBLANES)
            for r in range(rows_per):
                gather_copy(gather_src(i + 1, row0 + r), row0 + r, nxt).start()
                scatter_copy(scatter_dst(i - 1, row0 + r), row0 + r, nxt).start()
            acc[...] += jnp.dot(mid_s[...], wd_ref[jnp.maximum(k - 1, 0)],
                                preferred_element_type=F32)
            hn = hn_s[...]
            gate = jnp.dot(hn, wg_ref[k], preferred_element_type=F32)
            up = jnp.dot(hn, wu_ref[k], preferred_element_type=F32)
            mid_s[...] = (gate * jax.nn.sigmoid(gate) * up).astype(BF16)
            return carry

        lax.fori_loop(0, n_chunks, chunk, 0)
        wait_scatters(nxt)
        y = acc[...] + jnp.dot(mid_s[...], wd_ref[n_chunks - 1], preferred_element_type=F32)
        _store_row_tiles(ybuf.at[cur], y)

    @pl.when(i == n_used)
    def _():
        wait_gathers(cur)
        for r in range(tm):
            scatter_copy(scatter_dst(i - 1, r), r, nxt).start()
        wait_scatters(nxt)


def _experts_sparse(xt, wg, wu, wd, slot, tile_expert, n_used, tm, n_tiles):
    n_tok = xt.shape[0] // SUBLANES
    _, n_chunks, d, fc = wg.shape
    n_pad = (n_tiles + 2) * tm - 2 * n_tok
    assert n_tok <= 1 << TOK_BITS and 2 * n_tok + n_pad <= 1 << (31 - TOK_BITS)
    assert tm % (n_chunks * SUBLANES) == 0
    w_idx = lambda i, slot, te, nu: (te[jnp.minimum(i, n_tiles - 1)], 0, 0, 0)
    grid_spec = pltpu.PrefetchScalarGridSpec(
        num_scalar_prefetch=3,
        grid=(n_tiles + 1,),
        in_specs=[pl.BlockSpec(memory_space=pl.ANY),
                  pl.BlockSpec((None, n_chunks, d, fc), w_idx),
                  pl.BlockSpec((None, n_chunks, d, fc), w_idx),
                  pl.BlockSpec((None, n_chunks, fc, d), w_idx)],
        out_specs=pl.BlockSpec(memory_space=pl.ANY),
        scratch_shapes=[pltpu.VMEM((2, tm * SUBLANES, LANES), F32),
                        pltpu.VMEM((2, tm * SUBLANES, LANES), F32),
                        pltpu.VMEM((tm, d), BF16), pltpu.VMEM((tm, fc), BF16),
                        pltpu.VMEM((tm, d), F32),
                        pltpu.SemaphoreType.DMA((2,)), pltpu.SemaphoreType.DMA((2,)),
                        pltpu.SemaphoreType.DMA((1,))])
    return pl.pallas_call(
        functools.partial(_experts_body, tm),
        grid_spec=grid_spec,
        out_shape=jax.ShapeDtypeStruct(((2 * n_tok + n_pad) * SUBLANES, LANES), F32),
        compiler_params=_cparams(("arbitrary",)),
        name="experts_sparse",
    )(slot, tile_expert, n_used, xt, wg, wu, wd)


def _combine_body(final, x_ref, y1_ref, y2_ref, rec_ref, *rest):
    o_ref = rest[-1]
    tm = x_ref.shape[0]
    rec = rec_ref[...]
    out = (x_ref[...] + rec[:, R_G1:R_G1 + 1] * _load_row_tiles(y1_ref, tm)
           + rec[:, R_G2:R_G2 + 1] * _load_row_tiles(y2_ref, tm))
    o_ref[...] = _rms(out, rest[0][...]) if final else out


def _combine(x, y, rec, tm, final_g=None):
    n, d = x.shape
    final = final_g is not None
    nb = n // tm
    row = pl.BlockSpec((tm, d), lambda i: (i, 0))
    in_specs = [row, pl.BlockSpec((tm * SUBLANES, LANES), lambda i: (i, 0)),
                pl.BlockSpec((tm * SUBLANES, LANES), lambda i: (nb + i, 0)),
                pl.BlockSpec((tm, LANES), lambda i: (i, 0))]
    args = [x, y, y, rec]
    if final:
        in_specs.append(pl.BlockSpec((1, d), lambda i: (0, 0)))
        args.append(final_g)
    return pl.pallas_call(
        functools.partial(_combine_body, final),
        grid=(n // tm,),
        in_specs=in_specs,
        out_specs=row,
        out_shape=jax.ShapeDtypeStruct((n, d), F32),
        compiler_params=_cparams(("parallel",)),
        name="moe_combine",
    )(*args)


def _ffn_body(n_chunks, per_expert, final, *refs):
    routed = per_expert is not None
    x_ref, g_ref, wg_ref, wu_ref, wd_ref = refs[:5]
    rest = list(refs[5:])
    cw_ref = rest.pop(0) if routed else None
    fg_ref = rest.pop(0) if final else None
    o_ref, hn_s = rest
    j = pl.program_id(1)

    @pl.when(j == 0)
    def _():
        hn_s[...] = _rms(x_ref[...], g_ref[...]).astype(BF16)
        o_ref[...] = x_ref[...]

    hn = hn_s[...]
    gate = jnp.dot(hn, wg_ref[...], preferred_element_type=F32)
    up = jnp.dot(hn, wu_ref[...], preferred_element_type=F32)
    mid = (gate * jax.nn.sigmoid(gate) * up).astype(BF16)
    y = jnp.dot(mid, wd_ref[...], preferred_element_type=F32)
    if routed:
        cw = cw_ref[...]
        lane = lax.broadcasted_iota(jnp.int32, cw.shape, 1)
        y = y * jnp.sum(jnp.where(lane == j // per_expert, cw, 0.0), axis=-1, keepdims=True)
    o_ref[...] += y

    if final:
        @pl.when(j == n_chunks - 1)
        def _():
            o_ref[...] = _rms(o_ref[...], fg_ref[...])


def _ffn(x, g, wg, wu, wd, tm, cw=None, final_g=None):
    n, d = x.shape
    routed = cw is not None
    final = final_g is not None
    if routed:
        n_e, per_expert, _, fc = wg.shape
        n_chunks = n_e * per_expert
        w_idx = lambda i, j: (j // per_expert, j % per_expert, 0, 0)
        w_in_spec = pl.BlockSpec((None, None, d, fc), w_idx)
        w_dn_spec = pl.BlockSpec((None, None, fc, d), w_idx)
    else:
        fc = D_FF_CHUNK
        per_expert = None
        n_chunks = wg.shape[1] // fc
        w_in_spec = pl.BlockSpec((d, fc), lambda i, j: (0, j))
        w_dn_spec = pl.BlockSpec((fc, d), lambda i, j: (j, 0))
    row = pl.BlockSpec((tm, d), lambda i, j: (i, 0))
    vec = pl.BlockSpec((1, d), lambda i, j: (0, 0))
    in_specs = [row, vec, w_in_spec, w_in_spec, w_dn_spec]
    args = [x, g, wg, wu, wd]
    if routed:
        in_specs.append(pl.BlockSpec((tm, LANES), lambda i, j: (i, 0)))
        args.append(cw)
    if final:
        in_specs.append(vec)
        args.append(final_g)
    return pl.pallas_call(
        functools.partial(_ffn_body, n_chunks, per_expert, final),
        grid=(n // tm, n_chunks),
        in_specs=in_specs,
        out_specs=row,
        out_shape=jax.ShapeDtypeStruct((n, d), F32),
        scratch_shapes=[pltpu.VMEM((tm, d), BF16)],
        compiler_params=_cparams(("parallel", "arbitrary")),
        name="ffn",
    )(*args)


def _block_diag(w):
    h, a, b = w.shape
    eye = jnp.eye(h, dtype=w.dtype)
    return (eye[:, None, :, None] * w[:, :, None, :]).reshape(h * a, h * b)


def _layer_weights(l, P):
    row = lambda v: v[l].reshape(1, -1)
    return dict(
        norm_mix=row(P["norm_mix"]), w_in=P["w_in"][l].astype(BF16),
        pool_w=P["pool_w"][l].astype(BF16), pool_scale=row(P["pool_scale"]),
        conv_w=P["conv_w"][l], conv_b=row(P["conv_b"]),
        gate_a_w=_block_diag(P["gate_a_w"][l]).astype(BF16), gate_a_b=row(P["gate_a_b"]),
        gate_x_w=_block_diag(P["gate_x_w"][l]).astype(BF16), gate_x_b=row(P["gate_x_b"]),
        lam=row(P["rglru_lambda"]),
        out_norm_pool=row(P["out_norm_pool"]), out_norm_rnn=row(P["out_norm_rnn"]),
        w_out=P["w_out"][l].astype(BF16), norm_mem=row(P["norm_mem"]),
        w_mq=P["w_mq"][l].astype(BF16), w_mo=P["w_mo"][l].astype(BF16),
        norm_ffn=row(P["norm_ffn"]),
        moe=(_chunk_experts(P["moe_w_gate"][l // 2], P["moe_w_up"][l // 2], P["moe_w_down"][l // 2])
             if l % 2 == 1 else None))


def _channel_mixer(x, l, lw, P, tm, final_g, sparse):
    j = l // 2
    n = x.shape[0]
    if l % 2 == 0:
        return _ffn(x, lw["norm_ffn"], P["ffn_w_gate"][j].astype(BF16),
                    P["ffn_w_up"][j].astype(BF16), P["ffn_w_down"][j].astype(BF16),
                    tm, final_g=final_g)
    rw = jnp.pad(P["router_w"][j], ((0, 0), (0, LANES - N_EXPERTS)))
    wg, wu, wd = lw["moe"]
    if not sparse:
        cw = _router(x, lw["norm_ffn"], rw, tm)
        return _ffn(x, lw["norm_ffn"], wg, wu, wd, tm, cw=cw, final_g=final_g)
    te = MOE_TILE
    n_tiles = -(-2 * n // te) + N_EXPERTS
    rec, rows, cnt, xt = _router_rank(x, lw["norm_ffn"], rw, tm)
    counts = cnt[0, :N_EXPERTS].astype(jnp.int32)
    slot, tile_expert, n_used = _slot_map(counts, rows, n, te, n_tiles, SLOT_MAP_CHUNK)
    y = _experts_sparse(xt, wg, wu, wd, slot, tile_expert, n_used, te, n_tiles)
    return _combine(x, y, rec, tm, final_g=final_g)


def _chunk_experts(wg, wu, wd):
    e, d, f = wg.shape
    c = -(-f // MOE_CHUNK)
    pad = c * MOE_CHUNK - f

    def cols(w):
        w = jnp.pad(w, ((0, 0), (0, 0), (0, pad))).reshape(e, d, c, MOE_CHUNK)
        return jnp.transpose(w, (0, 2, 1, 3)).astype(BF16)

    rows = jnp.pad(wd, ((0, 0), (0, pad), (0, 0))).reshape(e, c, MOE_CHUNK, d).astype(BF16)
    return cols(wg), cols(wu), rows


def kernel(x_prompt, x_sample, mem_prompt, state_pool, state_conv, state_h, cache_mem_k, cache_mem_v,
           norm_mix, w_in, pool_w, pool_scale, conv_w, conv_b, gate_a_w, gate_a_b, gate_x_w, gate_x_b,
           rglru_lambda, out_norm_pool, out_norm_rnn, w_out, norm_mem, mem_norm, w_mq, w_mk, w_mv, w_mo,
           norm_ffn, ffn_w_gate, ffn_w_up, ffn_w_down, router_w, moe_w_gate, moe_w_up, moe_w_down,
           final_norm):
    P = dict(norm_mix=norm_mix, w_in=w_in, pool_w=pool_w, pool_scale=pool_scale, conv_w=conv_w,
             conv_b=conv_b, gate_a_w=gate_a_w, gate_a_b=gate_a_b, gate_x_w=gate_x_w,
             gate_x_b=gate_x_b, rglru_lambda=rglru_lambda, out_norm_pool=out_norm_pool,
             out_norm_rnn=out_norm_rnn, w_out=w_out, norm_mem=norm_mem, w_mq=w_mq, w_mo=w_mo,
             norm_ffn=norm_ffn, ffn_w_gate=ffn_w_gate, ffn_w_up=ffn_w_up, ffn_w_down=ffn_w_down,
             router_w=router_w, moe_w_gate=moe_w_gate, moe_w_up=moe_w_up, moe_w_down=moe_w_down)
    bsz, t_len, d = x_prompt.shape
    n_dec = x_sample.shape[0]
    n_tok = bsz * t_len
    fg = final_norm.reshape(1, d)
    TM, TT, TM_DEC, BB = 512, 512, n_dec, SUBLANES

    xp = x_prompt
    xs = x_sample.reshape(n_dec, d)
    mem2 = mem_prompt.reshape(bsz * N_MEM, d)
    mem_ks, mem_vs, pools_p, convs_p, hs_p, pools_s, convs_s, hs_s = ([] for _ in range(8))
    zeros_hist = jnp.zeros((bsz, POOL_PAD, D_POOL), F32)
    zeros_conv = jnp.zeros((bsz, CONV_PAD, D_RNN), F32)
    zeros_h = jnp.zeros((bsz, 1, D_RNN), F32)

    for l in range(DEPTH):
        lw = _layer_weights(l, P)
        last_g = fg if l == DEPTH - 1 else None

        mg = mem_norm[l].reshape(1, d)
        mk = _norm_matmul(mem2, mg, w_mk[l].astype(BF16), TM).reshape(bsz, N_MEM, d)
        mv = _norm_matmul(mem2, mg, w_mv[l].astype(BF16), TM).reshape(bsz, N_MEM, d)
        mem_ks.append(mk.reshape(bsz, N_MEM, MEM_HEADS, MEM_HEAD_DIM))
        mem_vs.append(mv.reshape(bsz, N_MEM, MEM_HEADS, MEM_HEAD_DIM))

        proj = _norm_matmul(xp.reshape(n_tok, d), lw["norm_mix"], lw["w_in"], TM)
        proj = proj.reshape(bsz, t_len, D_IN)
        mixed, h_last = _mixer_prompt(proj, zeros_hist, zeros_conv, zeros_h, lw, 0, TT)
        pools_p.append(proj[:, t_len - POOL_HIST:, :D_POOL])
        convs_p.append(proj[:, t_len - (CONV_WIDTH - 1):, D_POOL:D_MIX])
        hs_p.append(h_last[:, 0, :])
        x2 = _matmul_res(mixed.reshape(n_tok, D_MIX), lw["w_out"], xp.reshape(n_tok, d), TM)
        x3 = _attn_prompt(x2.reshape(bsz, t_len, d), mk, mv, lw["norm_mem"], lw["w_mq"],
                          lw["w_mo"], TT)
        xp = _channel_mixer(x3.reshape(n_tok, d), l, lw, P, TM, last_g, True)
        xp = xp.reshape(bsz, t_len, d)

        proj_s = _norm_matmul(xs, lw["norm_mix"], lw["w_in"], TM_DEC)
        mixed_s, h_new = _mixer_decode(proj_s, jnp.swapaxes(state_pool[l], 0, 1),
                                       jnp.swapaxes(state_conv[l], 0, 1), state_h[l], lw,
                                       PAST_LEN)
        pools_s.append(jnp.concatenate([state_pool[l][:, 1:], proj_s[:, None, :D_POOL]], axis=1))
        convs_s.append(jnp.concatenate([state_conv[l][:, 1:], proj_s[:, None, D_POOL:D_MIX]],
                                       axis=1))
        hs_s.append(h_new)
        xs2 = _matmul_res(mixed_s, lw["w_out"], xs, TM_DEC)
        q_s = _norm_matmul(xs2, lw["norm_mem"], lw["w_mq"], TM_DEC)
        att = _attn_decode(q_s.reshape(n_dec, MEM_HEADS, MEM_HEAD_DIM), cache_mem_k, cache_mem_v,
                           l, BB)
        xs3 = _matmul_res(att.reshape(n_dec, d), lw["w_mo"], xs2, TM_DEC)
        xs = _channel_mixer(xs3, l, lw, P, TM_DEC, last_g, False)

    return (xp, xs.reshape(n_dec, 1, d), jnp.stack(pools_p), jnp.stack(convs_p), jnp.stack(hs_p),
            jnp.stack(mem_ks), jnp.stack(mem_vs), jnp.stack(pools_s), jnp.stack(convs_s),
            jnp.stack(hs_s))
```

```python
import functools
import math

import jax
import jax.numpy as jnp
from jax import lax
from jax.experimental import pallas as pl
from jax.experimental.pallas import tpu as pltpu

F32 = jnp.float32
BF16 = jnp.bfloat16

D_MODEL = 1024
DEPTH = 2
PAST_LEN = 16384
D_POOL = 512
POOL_WINDOWS = (2, 4, 8, 16)
POOL_GROUP = 128
POOL_HIST = 15
D_RNN = 512
RNN_HEADS = 8
CONV_WIDTH = 4
RGLRU_C = 8.0
D_MIX = 1024
D_IN = 1536
N_MEM = 256
MEM_HEADS = 4
MEM_HEAD_DIM = 256
N_EXPERTS = 8
D_FF_CHUNK = 1408
EPS = 1e-6

SUBLANES = 8
LANES = 128
POOL_PAD = 16
CONV_PAD = 8
VMEM_LIMIT = 56 * 1024 * 1024
SLOT_MAP_CHUNK = 2048
TOK_BITS = 14
MOE_CHUNK = 256
MOE_TILE = 576


def _cparams(sem):
    return pltpu.CompilerParams(dimension_semantics=sem, vmem_limit_bytes=VMEM_LIMIT)


def _rms(x, g):
    ms = jnp.mean(x * x, axis=-1, keepdims=True)
    return x * lax.rsqrt(ms + EPS) * g


def _gelu_tanh(x):
    c = math.sqrt(2.0 / math.pi)
    return 0.5 * x * (1.0 + jnp.tanh(c * (x + 0.044715 * (x * x * x))))


def _softplus(x):
    return jnp.maximum(x, 0.0) + jnp.log1p(jnp.exp(-jnp.abs(x)))


def _norm_matmul_body(x_ref, g_ref, w_ref, o_ref):
    hn = _rms(x_ref[...], g_ref[...]).astype(BF16)
    o_ref[...] = jnp.dot(hn, w_ref[...], preferred_element_type=F32)


def _norm_matmul(x, g, w, tm):
    n, d = x.shape
    m = w.shape[1]
    return pl.pallas_call(
        _norm_matmul_body,
        grid=(n // tm,),
        in_specs=[pl.BlockSpec((tm, d), lambda i: (i, 0)),
                  pl.BlockSpec((1, d), lambda i: (0, 0)),
                  pl.BlockSpec((d, m), lambda i: (0, 0))],
        out_specs=pl.BlockSpec((tm, m), lambda i: (i, 0)),
        out_shape=jax.ShapeDtypeStruct((n, m), F32),
        compiler_params=_cparams(("parallel",)),
        name="norm_matmul",
    )(x, g, w)


def _matmul_res_body(a_ref, w_ref, r_ref, o_ref):
    o_ref[...] = r_ref[...] + jnp.dot(a_ref[...].astype(BF16), w_ref[...],
                                      preferred_element_type=F32)


def _matmul_res(a, w, res, tm):
    n, k = a.shape
    m = w.shape[1]
    return pl.pallas_call(
        _matmul_res_body,
        grid=(n // tm,),
        in_specs=[pl.BlockSpec((tm, k), lambda i: (i, 0)),
                  pl.BlockSpec((k, m), lambda i: (0, 0)),
                  pl.BlockSpec((tm, m), lambda i: (i, 0))],
        out_specs=pl.BlockSpec((tm, m), lambda i: (i, 0)),
        out_shape=jax.ShapeDtypeStruct((n, m), F32),
        compiler_params=_cparams(("parallel",)),
        name="matmul_res",
    )(a, w, res)


def _pool_project(z_groups, pw_ref, scale):
    ys = [jnp.dot(z.astype(BF16), pw_ref[g], preferred_element_type=F32)
          for g, z in enumerate(z_groups)]
    return jnp.concatenate(ys, axis=-1) * scale


def _rglru_terms(c, first_pos, wa_ref, ba_ref, wx_ref, bx_ref, lam_ref):
    cb = c.astype(BF16)
    r = jax.nn.sigmoid(jnp.dot(cb, wa_ref[...], preferred_element_type=F32) + ba_ref[...])
    i = jax.nn.sigmoid(jnp.dot(cb, wx_ref[...], preferred_element_type=F32) + bx_ref[...])
    log_a = (-RGLRU_C) * r * _softplus(-lam_ref[...])
    a = jnp.exp(log_a)
    mult = jnp.sqrt(1.0 - a * a)
    if first_pos is not None:
        mult = jnp.where(first_pos, 1.0, mult)
    return a, mult * (i * c)


def _mix_out(y_pool, h, u_gate, gp_ref, gr_ref):
    y_rnn = h * _gelu_tanh(u_gate)
    return jnp.concatenate([_rms(y_pool, gp_ref[...]), _rms(y_rnn, gr_ref[...])], axis=-1)


def _mixer_body(pos0, tt, proj_ref, hist_ref, conv_ref, h0_ref, pw_ref, ps_ref, cw_ref, cb_ref,
                wa_ref, ba_ref, wx_ref, bx_ref, lam_ref, gp_ref, gr_ref,
                mix_ref, hl_ref, pbuf, cbuf, a_s, b_s, hc):
    t = pl.program_id(1)

    @pl.when(t == 0)
    def _():
        pbuf[0:POOL_PAD, :] = hist_ref[...]
        cbuf[0:CONV_PAD, :] = conv_ref[...]
        hc[...] = jnp.broadcast_to(h0_ref[...], (SUBLANES, D_RNN))

    @pl.when(t > 0)
    def _():
        pbuf[0:POOL_PAD, :] = pbuf[tt:tt + POOL_PAD, :]
        cbuf[0:CONV_PAD, :] = cbuf[tt:tt + CONV_PAD, :]

    u_pool = proj_ref[:, 0:D_POOL]
    pbuf[POOL_PAD:POOL_PAD + tt, :] = u_pool
    cbuf[CONV_PAD:CONV_PAD + tt, :] = proj_ref[:, D_POOL:D_MIX]

    pos = pos0 + t * tt + lax.broadcasted_iota(jnp.int32, (tt, 1), 0)

    zs = []
    for g, w in enumerate(POOL_WINDOWS):
        sl = slice(g * POOL_GROUP, (g + 1) * POOL_GROUP)
        acc = pbuf[POOL_PAD:POOL_PAD + tt, sl]
        for k in range(1, w):
            acc = acc + pbuf[POOL_PAD - k:POOL_PAD - k + tt, sl]
        cnt = jnp.minimum(pos + 1, w).astype(F32)
        zs.append(acc / cnt - u_pool[:, sl])
    y_pool = _pool_project(zs, pw_ref, ps_ref[...])

    c = cb_ref[...]
    for k in range(CONV_WIDTH):
        off = CONV_PAD - (CONV_WIDTH - 1) + k
        c = c + cbuf[off:off + tt, :] * cw_ref[k:k + 1, :]

    a, b = _rglru_terms(c, pos == 0, wa_ref, ba_ref, wx_ref, bx_ref, lam_ref)
    a_s[...] = a
    b_s[...] = b

    row = lax.broadcasted_iota(jnp.int32, (SUBLANES, D_RNN), 0)

    def group(j, carry):
        r0 = pl.multiple_of(j * SUBLANES, SUBLANES)
        ga = a_s[pl.ds(r0, SUBLANES), :]
        gb = b_s[pl.ds(r0, SUBLANES), :]
        for s in (1, 2, 4):
            keep = row >= s
            gb = jnp.where(keep, ga * pltpu.roll(gb, s, axis=0) + gb, gb)
            ga = jnp.where(keep, ga * pltpu.roll(ga, s, axis=0), ga)
        h = ga * carry + gb
        b_s[pl.ds(r0, SUBLANES), :] = h
        return jnp.broadcast_to(h[SUBLANES - 1:SUBLANES, :], (SUBLANES, D_RNN))

    carry = lax.fori_loop(0, tt // SUBLANES, group, hc[...])
    hc[...] = carry
    hl_ref[...] = carry

    mix = _mix_out(y_pool, b_s[...], proj_ref[:, D_MIX:D_IN], gp_ref, gr_ref)
    mix_ref[...] = mix.astype(BF16)


def _mixer_prompt(proj, hist, conv, h0, lw, pos0, tt):
    bsz, t_len, _ = proj.shape
    full = lambda shape: pl.BlockSpec(shape, lambda b, t: (0,) * len(shape))
    per_b = lambda rows, cols: pl.BlockSpec((None, rows, cols), lambda b, t: (b, 0, 0))
    return pl.pallas_call(
        functools.partial(_mixer_body, pos0, tt),
        grid=(bsz, t_len // tt),
        in_specs=[pl.BlockSpec((None, tt, D_IN), lambda b, t: (b, t, 0)),
                  per_b(POOL_PAD, D_POOL), per_b(CONV_PAD, D_RNN), per_b(1, D_RNN),
                  full((len(POOL_WINDOWS), POOL_GROUP, POOL_GROUP)), full((1, D_POOL)),
                  full((CONV_WIDTH, D_RNN)), full((1, D_RNN)),
                  full((D_RNN, D_RNN)), full((1, D_RNN)),
                  full((D_RNN, D_RNN)), full((1, D_RNN)),
                  full((1, D_RNN)), full((1, D_POOL)), full((1, D_RNN))],
        out_specs=[pl.BlockSpec((None, tt, D_MIX), lambda b, t: (b, t, 0)),
                   per_b(SUBLANES, D_RNN)],
        out_shape=[jax.ShapeDtypeStruct((bsz, t_len, D_MIX), BF16),
                   jax.ShapeDtypeStruct((bsz, SUBLANES, D_RNN), F32)],
        scratch_shapes=[pltpu.VMEM((POOL_PAD + tt, D_POOL), F32),
                        pltpu.VMEM((CONV_PAD + tt, D_RNN), F32),
                        pltpu.VMEM((tt, D_RNN), F32),
                        pltpu.VMEM((tt, D_RNN), F32),
                        pltpu.VMEM((SUBLANES, D_RNN), F32)],
        compiler_params=_cparams(("parallel", "arbitrary")),
        name="mixer_prompt",
    )(proj, hist, conv, h0, lw["pool_w"], lw["pool_scale"], lw["conv_w"], lw["conv_b"],
      lw["gate_a_w"], lw["gate_a_b"], lw["gate_x_w"], lw["gate_x_b"], lw["lam"],
      lw["out_norm_pool"], lw["out_norm_rnn"])


def _mixer_decode_body(pos0, proj_ref, hist_ref, conv_ref, h0_ref, pw_ref, ps_ref, cw_ref, cb_ref,
                       wa_ref, ba_ref, wx_ref, bx_ref, lam_ref, gp_ref, gr_ref, mix_ref, hn_ref):
    u_pool = proj_ref[:, 0:D_POOL]
    zs = []
    for g, w in enumerate(POOL_WINDOWS):
        sl = slice(g * POOL_GROUP, (g + 1) * POOL_GROUP)
        acc = u_pool[:, sl]
        for k in range(1, w):
            acc = acc + hist_ref[POOL_HIST - k, :, sl]
        zs.append(acc / float(min(pos0 + 1, w)) - u_pool[:, sl])
    y_pool = _pool_project(zs, pw_ref, ps_ref[...])

    c = cb_ref[...]
    for k in range(CONV_WIDTH - 1):
        c = c + conv_ref[k] * cw_ref[k:k + 1, :]
    c = c + proj_ref[:, D_POOL:D_MIX] * cw_ref[CONV_WIDTH - 1:CONV_WIDTH, :]

    a, b = _rglru_terms(c, None, wa_ref, ba_ref, wx_ref, bx_ref, lam_ref)
    h = a * h0_ref[...] + b
    hn_ref[...] = h
    mix_ref[...] = _mix_out(y_pool, h, proj_ref[:, D_MIX:D_IN], gp_ref, gr_ref).astype(BF16)


def _mixer_decode(proj, hist_t, conv_t, h0, lw, pos0):
    assert pos0 > 0
    n = proj.shape[0]
    vm = pl.BlockSpec(memory_space=pltpu.VMEM)
    return pl.pallas_call(
        functools.partial(_mixer_decode_body, pos0),
        in_specs=[vm] * 15,
        out_specs=[vm, vm],
        out_shape=[jax.ShapeDtypeStruct((n, D_MIX), BF16),
                   jax.ShapeDtypeStruct((n, D_RNN), F32)],
        compiler_params=pltpu.CompilerParams(vmem_limit_bytes=VMEM_LIMIT),
        name="mixer_decode",
    )(proj, hist_t, conv_t, h0, lw["pool_w"], lw["pool_scale"], lw["conv_w"], lw["conv_b"],
      lw["gate_a_w"], lw["gate_a_b"], lw["gate_x_w"], lw["gate_x_b"], lw["lam"],
      lw["out_norm_pool"], lw["out_norm_rnn"])


def _softmax_rows(s):
    e = jnp.exp(s - jnp.max(s, axis=-1, keepdims=True))
    return e / jnp.sum(e, axis=-1, keepdims=True)


_NT = (((1,), (1,)), ((), ()))


def _attn_prompt_body(x_ref, k_ref, v_ref, g_ref, wq_ref, wo_ref, o_ref):
    x = x_ref[...]
    xn = _rms(x, g_ref[...]).astype(BF16)
    q = jnp.dot(xn, wq_ref[...], preferred_element_type=F32)
    outs = []
    for h in range(MEM_HEADS):
        sl = slice(h * MEM_HEAD_DIM, (h + 1) * MEM_HEAD_DIM)
        s = lax.dot_general(q[:, sl].astype(BF16), k_ref[:, sl].astype(BF16), _NT,
                            preferred_element_type=F32) * (MEM_HEAD_DIM ** -0.5)
        p = _softmax_rows(s).astype(BF16)
        outs.append(jnp.dot(p, v_ref[:, sl].astype(BF16), preferred_element_type=F32))
    o = jnp.concatenate(outs, axis=-1).astype(BF16)
    o_ref[...] = x + jnp.dot(o, wo_ref[...], preferred_element_type=F32)


def _attn_prompt(x, k, v, g, wq, wo, tt):
    bsz, t_len, d = x.shape
    full = lambda shape: pl.BlockSpec(shape, lambda b, t: (0,) * len(shape))
    kv = pl.BlockSpec((None, N_MEM, d), lambda b, t: (b, 0, 0))
    xs = pl.BlockSpec((None, tt, d), lambda b, t: (b, t, 0))
    return pl.pallas_call(
        _attn_prompt_body,
        grid=(bsz, t_len // tt),
        in_specs=[xs, kv, kv, full((1, d)), full((d, d)), full((d, d))],
        out_specs=xs,
        out_shape=jax.ShapeDtypeStruct(x.shape, F32),
        compiler_params=_cparams(("parallel", "parallel")),
        name="attn_prompt",
    )(x, k, v, g, wq, wo)


def _attn_decode_body(bb, q_ref, k_ref, v_ref, o_ref):
    for b in range(bb):
        q = q_ref[b]
        s = jnp.sum(k_ref[b] * q[None], axis=-1, keepdims=True) * (MEM_HEAD_DIM ** -0.5)
        e = jnp.exp(s - jnp.max(s, axis=0, keepdims=True))
        p = e / jnp.sum(e, axis=0, keepdims=True)
        o_ref[b] = jnp.sum(p * v_ref[b], axis=0)


def _attn_decode(q, k, v, layer, bb):
    n, nh, hd = q.shape
    kv = pl.BlockSpec((None, bb, N_MEM, nh, hd), lambda i: (layer, i, 0, 0, 0))
    qs = pl.BlockSpec((bb, nh, hd), lambda i: (i, 0, 0))
    return pl.pallas_call(
        functools.partial(_attn_decode_body, bb),
        grid=(n // bb,),
        in_specs=[qs, kv, kv],
        out_specs=qs,
        out_shape=jax.ShapeDtypeStruct((n, nh, hd), F32),
        compiler_params=_cparams(("parallel",)),
        name="attn_decode",
    )(q, k, v)


def _top2(hn, rw_ref):
    logits = jnp.dot(hn, rw_ref[...], precision=lax.Precision.HIGHEST,
                     preferred_element_type=F32)
    lane = lax.broadcasted_iota(jnp.int32, logits.shape, 1)
    neg = -jnp.inf
    lg = jnp.where(lane < N_EXPERTS, logits, neg)
    m1 = jnp.max(lg, axis=-1, keepdims=True)
    i1 = jnp.min(jnp.where(lg == m1, lane, LANES), axis=-1, keepdims=True)
    lg2 = jnp.where(lane == i1, neg, lg)
    m2 = jnp.max(lg2, axis=-1, keepdims=True)
    i2 = jnp.min(jnp.where(lg2 == m2, lane, LANES), axis=-1, keepdims=True)
    e2 = jnp.exp(m2 - m1)
    den = 1.0 + e2
    return lane, i1, i2, 1.0 / den, e2 / den


def _router_body(x_ref, g_ref, rw_ref, cw_ref):
    lane, i1, i2, g1, g2 = _top2(_rms(x_ref[...], g_ref[...]), rw_ref)
    cw_ref[...] = jnp.where(lane == i1, g1, 0.0) + jnp.where(lane == i2, g2, 0.0)


def _router(x, g, rw, tm):
    n, d = x.shape
    return pl.pallas_call(
        _router_body,
        grid=(n // tm,),
        in_specs=[pl.BlockSpec((tm, d), lambda i: (i, 0)),
                  pl.BlockSpec((1, d), lambda i: (0, 0)),
                  pl.BlockSpec((d, LANES), lambda i: (0, 0))],
        out_specs=pl.BlockSpec((tm, LANES), lambda i: (i, 0)),
        out_shape=jax.ShapeDtypeStruct((n, LANES), F32),
        compiler_params=_cparams(("parallel",)),
        name="router",
    )(x, g, rw)


def _store_row_tiles(ref, val):
    rows = val.shape[0]
    for c in range(val.shape[1] // LANES):
        ref[pl.ds(c, rows, stride=SUBLANES), :] = val[:, c * LANES:(c + 1) * LANES]


def _load_row_tiles(ref, rows):
    return jnp.concatenate([ref[pl.ds(c, rows, stride=SUBLANES), :] for c in range(SUBLANES)],
                           axis=-1)


R_E1, R_E2, R_RANK1, R_RANK2, R_G1, R_G2 = range(6)


def _router_rank_body(x_ref, g_ref, rw_ref, rec_ref, rows_ref, cnt_ref, hn_ref, carry):
    i = pl.program_id(0)
    tm = x_ref.shape[0]

    @pl.when(i == 0)
    def _():
        carry[...] = jnp.zeros_like(carry)

    hn = _rms(x_ref[...], g_ref[...])
    _store_row_tiles(hn_ref, hn)
    lane, i1, i2, g1, g2 = _top2(hn, rw_ref)
    picks = ((lane == i1) | (lane == i2)).astype(F32)
    r_io = lax.broadcasted_iota(jnp.int32, (tm, tm), 0)
    c_io = lax.broadcasted_iota(jnp.int32, (tm, tm), 1)
    earlier = (c_io < r_io).astype(BF16)
    before = jnp.dot(earlier, picks.astype(BF16), preferred_element_type=F32) + carry[...]
    rank1 = jnp.sum(jnp.where(lane == i1, before, 0.0), axis=-1, keepdims=True)
    rank2 = jnp.sum(jnp.where(lane == i2, before, 0.0), axis=-1, keepdims=True)
    carry[...] += jnp.sum(picks, axis=0, keepdims=True)
    cnt_ref[...] = jnp.broadcast_to(carry[...], cnt_ref.shape)

    rec = jnp.zeros((tm, LANES), F32)
    for k, v in ((R_E1, i1.astype(F32)), (R_E2, i2.astype(F32)), (R_RANK1, rank1),
                 (R_RANK2, rank2), (R_G1, g1), (R_G2, g2)):
        rec = jnp.where(lane == k, v, rec)
    rec_ref[...] = rec
    rows_ref[...] = rec.T[0:SUBLANES, :].astype(jnp.int32)


def _router_rank(x, g, rw, tm):
    n, d = x.shape
    assert d == SUBLANES * LANES
    return pl.pallas_call(
        _router_rank_body,
        grid=(n // tm,),
        in_specs=[pl.BlockSpec((tm, d), lambda i: (i, 0)),
                  pl.BlockSpec((1, d), lambda i: (0, 0)),
                  pl.BlockSpec((d, LANES), lambda i: (0, 0))],
        out_specs=[pl.BlockSpec((tm, LANES), lambda i: (i, 0)),
                   pl.BlockSpec((SUBLANES, tm), lambda i: (0, i)),
                   pl.BlockSpec((SUBLANES, LANES), lambda i: (0, 0)),
                   pl.BlockSpec((tm * SUBLANES, LANES), lambda i: (i, 0))],
        out_shape=[jax.ShapeDtypeStruct((n, LANES), F32),
                   jax.ShapeDtypeStruct((SUBLANES, n), jnp.int32),
                   jax.ShapeDtypeStruct((SUBLANES, LANES), F32),
                   jax.ShapeDtypeStruct((n * SUBLANES, LANES), F32)],
        scratch_shapes=[pltpu.VMEM((1, LANES), F32)],
        compiler_params=_cparams(("arbitrary",)),
        name="router_rank",
    )(x, g, rw)


def _slot_map_body(n_tok, tm, n_tiles, cnt_ref, rows_ref, slot_ref, te_ref, nu_ref, off_s,
                   pos_v, pos1_s, pos2_s):
    c = pl.program_id(0)
    tc = rows_ref.shape[1]

    def fill_pad(s, ordinal):
        slot_ref[s] = (2 * n_tok + ordinal) << TOK_BITS
        return ordinal + 1

    @pl.when(c == 0)
    def _():
        lead = lax.fori_loop(0, tm, fill_pad, jnp.int32(0))

        def per_expert(e, carry):
            tile0, ordinal = carry
            n = cnt_ref[e]
            start = (tile0 + 1) * tm
            off_s[e] = start
            tile1 = tile0 + lax.div(n + (tm - 1), tm)

            def fill_tile(t, z):
                te_ref[t] = e
                return z

            lax.fori_loop(tile0, tile1, fill_tile, 0)
            ordinal = lax.fori_loop(start + n, (tile1 + 1) * tm, fill_pad, ordinal)
            return tile1, ordinal

        used, ordinal = lax.fori_loop(0, N_EXPERTS, per_expert, (jnp.int32(0), lead))
        nu_ref[0] = used

        def fill_rest(t, z):
            te_ref[t] = N_EXPERTS - 1
            return z

        lax.fori_loop(used, n_tiles, fill_rest, 0)
        lax.fori_loop((used + 1) * tm, (n_tiles + 2) * tm, fill_pad, ordinal)

    rows = rows_ref[...]
    pos = rows[R_RANK1:R_RANK2 + 1, :]
    for e in range(N_EXPERTS):
        pos = pos + jnp.where(rows[R_E1:R_E2 + 1, :] == e, off_s[e], 0)
    pos_v[...] = pos
    pltpu.sync_copy(pos_v.at[0], pos1_s)
    pltpu.sync_copy(pos_v.at[1], pos2_s)

    step = (1 << TOK_BITS) + 1

    def per_token(j, first):
        slot_ref[pos1_s[j]] = first
        slot_ref[pos2_s[j]] = first + (n_tok << TOK_BITS)
        return first + step

    lax.fori_loop(0, tc, per_token, (c * tc) * step, unroll=8)


def _slot_map(counts, rows, n_tok, tm, n_tiles, tc):
    smem = pl.BlockSpec(memory_space=pltpu.SMEM)
    return pl.pallas_call(
        functools.partial(_slot_map_body, n_tok, tm, n_tiles),
        grid=(n_tok // tc,),
        in_specs=[smem, pl.BlockSpec((SUBLANES, tc), lambda c: (0, c))],
        out_specs=[smem, smem, smem],
        out_shape=[jax.ShapeDtypeStruct(((n_tiles + 2) * tm,), jnp.int32),
                   jax.ShapeDtypeStruct((n_tiles,), jnp.int32),
                   jax.ShapeDtypeStruct((1,), jnp.int32)],
        scratch_shapes=[pltpu.SMEM((N_EXPERTS,), jnp.int32), pltpu.VMEM((2, tc), jnp.int32),
                        pltpu.SMEM((tc,), jnp.int32), pltpu.SMEM((tc,), jnp.int32)],
        compiler_params=_cparams(("arbitrary",)),
        name="slot_map",
    )(counts, rows)


def _experts_body(tm, slot_ref, te_ref, nu_ref, x_hbm, wg_ref, wu_ref, wd_ref,
                  out_hbm, xbuf, ybuf, hn_s, mid_s, acc, gsem, ssem, zsem):
    del te_ref
    i = pl.program_id(0)
    n_used = nu_ref[0]
    cur = i % 2
    nxt = 1 - cur

    def tile_rows(r):
        return pl.ds(pl.multiple_of(r * SUBLANES, SUBLANES), SUBLANES)

    def gather_copy(src_row, r, buf):
        return pltpu.make_async_copy(x_hbm.at[tile_rows(src_row), :],
                                     xbuf.at[buf, tile_rows(r), :], gsem.at[buf])

    def scatter_copy(dst_row, r, buf):
        return pltpu.make_async_copy(ybuf.at[buf, tile_rows(r), :],
                                     out_hbm.at[tile_rows(dst_row), :], ssem.at[buf])

    def gather_src(tile, r):
        return slot_ref[(tile + 1) * tm + r] & ((1 << TOK_BITS) - 1)

    def scatter_dst(tile, r):
        return slot_ref[(tile + 1) * tm + r] >> TOK_BITS

    def wait_gathers(buf):
        for r in range(tm):
            gather_copy(0, r, buf).wait()

    def wait_scatters(buf):
        for r in range(tm):
            scatter_copy(0, r, buf).wait()

    @pl.when(i == 0)
    def _():
        ybuf[...] = jnp.zeros_like(ybuf)

        def zero_tile(j, z):
            rows = pl.ds(pl.multiple_of(j * (tm * SUBLANES), SUBLANES), tm * SUBLANES)
            cp = pltpu.make_async_copy(ybuf.at[0], out_hbm.at[rows, :], zsem.at[0])
            cp.start()
            cp.wait()
            return z

        lax.fori_loop(n_used + 1, out_hbm.shape[0] // (tm * SUBLANES), zero_tile, 0)
        for r in range(tm):
            gather_copy(gather_src(0, r), r, 0).start()

    @pl.when(i < n_used)
    def _():
        wait_gathers(cur)
        hn_s[...] = _load_row_tiles(xbuf.at[cur], tm).astype(BF16)
        acc[...] = jnp.zeros_like(acc)
        mid_s[...] = jnp.zeros_like(mid_s)
        n_chunks = wg_ref.shape[0]
        rows_per = tm // n_chunks

        def chunk(k, carry):
            row0 = pl.multiple_of(k * rows_per, SUBLANES)
            gbase = (i + 2) * tm + row0
            sbase = i * tm + row0
            for r in range(rows_per):
                src = slot_ref[gbase + r] & ((1 << TOK_BITS) - 1)
                gather_copy(src, row0 + r, nxt).start()
                scatter_copy(slot_ref[sbase + r] >> TOK_BITS, row0 + r, nxt).start()
            acc[...] += jnp.dot(mid_s[...], wd_ref[jnp.maximum(k - 1, 0)],
                                preferred_element_type=F32)
            hn = hn_s[...]
            gate = jnp.dot(hn, wg_ref[k], preferred_element_type=F32)
            up = jnp.dot(hn, wu_ref[k], preferred_element_type=F32)
            mid_s[...] = (gate * jax.nn.sigmoid(gate) * up).astype(BF16)
            return carry

        lax.fori_loop(0, n_chunks, chunk, 0)
        wait_scatters(nxt)
        y = acc[...] + jnp.dot(mid_s[...], wd_ref[n_chunks - 1], preferred_element_type=F32)
        _store_row_tiles(ybuf.at[cur], y)

    @pl.when(i == n_used)
    def _():
        wait_gathers(cur)
        for r in range(tm):
            scatter_copy(scatter_dst(i - 1, r), r, nxt).start()
        wait_scatters(nxt)


def _experts_sparse(xt, wg, wu, wd, slot, tile_expert, n_used, tm, n_tiles):
    n_tok = xt.shape[0] // SUBLANES
    _, n_chunks, d, fc = wg.shape
    n_pad = (n_tiles + 2) * tm - 2 * n_tok
    assert n_tok <= 1 << TOK_BITS and 2 * n_tok + n_pad <= 1 << (31 - TOK_BITS)
    assert tm % (n_chunks * SUBLANES) == 0
    w_idx = lambda i, slot, te, nu: (te[jnp.minimum(i, n_tiles - 1)], 0, 0, 0)
    grid_spec = pltpu.PrefetchScalarGridSpec(
        num_scalar_prefetch=3,
        grid=(n_tiles + 1,),
        in_specs=[pl.BlockSpec(memory_space=pl.ANY),
                  pl.BlockSpec((None, n_chunks, d, fc), w_idx),
                  pl.BlockSpec((None, n_chunks, d, fc), w_idx),
                  pl.BlockSpec((None, n_chunks, fc, d), w_idx)],
        out_specs=pl.BlockSpec(memory_space=pl.ANY),
        scratch_shapes=[pltpu.VMEM((2, tm * SUBLANES, LANES), F32),
                        pltpu.VMEM((2, tm * SUBLANES, LANES), F32),
                        pltpu.VMEM((tm, d), BF16), pltpu.VMEM((tm, fc), BF16),
                        pltpu.VMEM((tm, d), F32),
                        pltpu.SemaphoreType.DMA((2,)), pltpu.SemaphoreType.DMA((2,)),
                        pltpu.SemaphoreType.DMA((1,))])
    return pl.pallas_call(
        functools.partial(_experts_body, tm),
        grid_spec=grid_spec,
        out_shape=jax.ShapeDtypeStruct(((2 * n_tok + n_pad) * SUBLANES, LANES), F32),
        compiler_params=_cparams(("arbitrary",)),
        name="experts_sparse",
    )(slot, tile_expert, n_used, xt, wg, wu, wd)


def _combine_body(final, x_ref, y1_ref, y2_ref, rec_ref, *rest):
    o_ref = rest[-1]
    tm = x_ref.shape[0]
    rec = rec_ref[...]
    out = (x_ref[...] + rec[:, R_G1:R_G1 + 1] * _load_row_tiles(y1_ref, tm)
           + rec[:, R_G2:R_G2 + 1] * _load_row_tiles(y2_ref, tm))
    o_ref[...] = _rms(out, rest[0][...]) if final else out


def _combine(x, y, rec, tm, final_g=None):
    n, d = x.shape
    final = final_g is not None
    nb = n // tm
    row = pl.BlockSpec((tm, d), lambda i: (i, 0))
    in_specs = [row, pl.BlockSpec((tm * SUBLANES, LANES), lambda i: (i, 0)),
                pl.BlockSpec((tm * SUBLANES, LANES), lambda i: (nb + i, 0)),
                pl.BlockSpec((tm, LANES), lambda i: (i, 0))]
    args = [x, y, y, rec]
    if final:
        in_specs.append(pl.BlockSpec((1, d), lambda i: (0, 0)))
        args.append(final_g)
    return pl.pallas_call(
        functools.partial(_combine_body, final),
        grid=(n // tm,),
        in_specs=in_specs,
        out_specs=row,
        out_shape=jax.ShapeDtypeStruct((n, d), F32),
        compiler_params=_cparams(("parallel",)),
        name="moe_combine",
    )(*args)


def _ffn_body(n_chunks, per_expert, final, *refs):
    routed = per_expert is not None
    x_ref, g_ref, wg_ref, wu_ref, wd_ref = refs[:5]
    rest = list(refs[5:])
    cw_ref = rest.pop(0) if routed else None
    fg_ref = rest.pop(0) if final else None
    o_ref, hn_s = rest
    j = pl.program_id(1)

    @pl.when(j == 0)
    def _():
        hn_s[...] = _rms(x_ref[...], g_ref[...]).astype(BF16)
        o_ref[...] = x_ref[...]

    hn = hn_s[...]
    gate = jnp.dot(hn, wg_ref[...], preferred_element_type=F32)
    up = jnp.dot(hn, wu_ref[...], preferred_element_type=F32)
    mid = (gate * jax.nn.sigmoid(gate) * up).astype(BF16)
    y = jnp.dot(mid, wd_ref[...], preferred_element_type=F32)
    if routed:
        cw = cw_ref[...]
        lane = lax.broadcasted_iota(jnp.int32, cw.shape, 1)
        y = y * jnp.sum(jnp.where(lane == j // per_expert, cw, 0.0), axis=-1, keepdims=True)
    o_ref[...] += y

    if final:
        @pl.when(j == n_chunks - 1)
        def _():
            o_ref[...] = _rms(o_ref[...], fg_ref[...])


def _ffn(x, g, wg, wu, wd, tm, cw=None, final_g=None):
    n, d = x.shape
    routed = cw is not None
    final = final_g is not None
    if routed:
        n_e, per_expert, _, fc = wg.shape
        n_chunks = n_e * per_expert
        w_idx = lambda i, j: (j // per_expert, j % per_expert, 0, 0)
        w_in_spec = pl.BlockSpec((None, None, d, fc), w_idx)
        w_dn_spec = pl.BlockSpec((None, None, fc, d), w_idx)
    else:
        fc = D_FF_CHUNK
        per_expert = None
        n_chunks = wg.shape[1] // fc
        w_in_spec = pl.BlockSpec((d, fc), lambda i, j: (0, j))
        w_dn_spec = pl.BlockSpec((fc, d), lambda i, j: (j, 0))
    row = pl.BlockSpec((tm, d), lambda i, j: (i, 0))
    vec = pl.BlockSpec((1, d), lambda i, j: (0, 0))
    in_specs = [row, vec, w_in_spec, w_in_spec, w_dn_spec]
    args = [x, g, wg, wu, wd]
    if routed:
        in_specs.append(pl.BlockSpec((tm, LANES), lambda i, j: (i, 0)))
        args.append(cw)
    if final:
        in_specs.append(vec)
        args.append(final_g)
    return pl.pallas_call(
        functools.partial(_ffn_body, n_chunks, per_expert, final),
        grid=(n // tm, n_chunks),
        in_specs=in_specs,
        out_specs=row,
        out_shape=jax.ShapeDtypeStruct((n, d), F32),
        scratch_shapes=[pltpu.VMEM((tm, d), BF16)],
        compiler_params=_cparams(("parallel", "arbitrary")),
        name="ffn",
    )(*args)


def _block_diag(w):
    h, a, b = w.shape
    eye = jnp.eye(h, dtype=w.dtype)
    return (eye[:, None, :, None] * w[:, :, None, :]).reshape(h * a, h * b)


def _layer_weights(l, P):
    row = lambda v: v[l].reshape(1, -1)
    return dict(
        norm_mix=row(P["norm_mix"]), w_in=P["w_in"][l].astype(BF16),
        pool_w=P["pool_w"][l].astype(BF16), pool_scale=row(P["pool_scale"]),
        conv_w=P["conv_w"][l], conv_b=row(P["conv_b"]),
        gate_a_w=_block_diag(P["gate_a_w"][l]).astype(BF16), gate_a_b=row(P["gate_a_b"]),
        gate_x_w=_block_diag(P["gate_x_w"][l]).astype(BF16), gate_x_b=row(P["gate_x_b"]),
        lam=row(P["rglru_lambda"]),
        out_norm_pool=row(P["out_norm_pool"]), out_norm_rnn=row(P["out_norm_rnn"]),
        w_out=P["w_out"][l].astype(BF16), norm_mem=row(P["norm_mem"]),
        w_mq=P["w_mq"][l].astype(BF16), w_mo=P["w_mo"][l].astype(BF16),
        norm_ffn=row(P["norm_ffn"]),
        moe=(_chunk_experts(P["moe_w_gate"][l // 2], P["moe_w_up"][l // 2], P["moe_w_down"][l // 2])
             if l % 2 == 1 else None))


def _channel_mixer(x, l, lw, P, tm, final_g, sparse):
    j = l // 2
    n = x.shape[0]
    if l % 2 == 0:
        return _ffn(x, lw["norm_ffn"], P["ffn_w_gate"][j].astype(BF16),
                    P["ffn_w_up"][j].astype(BF16), P["ffn_w_down"][j].astype(BF16),
                    tm, final_g=final_g)
    rw = jnp.pad(P["router_w"][j], ((0, 0), (0, LANES - N_EXPERTS)))
    wg, wu, wd = lw["moe"]
    if not sparse:
        cw = _router(x, lw["norm_ffn"], rw, tm)
        return _ffn(x, lw["norm_ffn"], wg, wu, wd, tm, cw=cw, final_g=final_g)
    te = MOE_TILE
    n_tiles = -(-2 * n // te) + N_EXPERTS
    rec, rows, cnt, xt = _router_rank(x, lw["norm_ffn"], rw, tm)
    counts = cnt[0, :N_EXPERTS].astype(jnp.int32)
    slot, tile_expert, n_used = _slot_map(counts, rows, n, te, n_tiles, SLOT_MAP_CHUNK)
    y = _experts_sparse(xt, wg, wu, wd, slot, tile_expert, n_used, te, n_tiles)
    return _combine(x, y, rec, tm, final_g=final_g)


def _chunk_experts(wg, wu, wd):
    e, d, f = wg.shape
    c = -(-f // MOE_CHUNK)
    pad = c * MOE_CHUNK - f

    def cols(w):
        w = jnp.pad(w, ((0, 0), (0, 0), (0, pad))).reshape(e, d, c, MOE_CHUNK)
        return jnp.transpose(w, (0, 2, 1, 3)).astype(BF16)

    rows = jnp.pad(wd, ((0, 0), (0, pad), (0, 0))).reshape(e, c, MOE_CHUNK, d).astype(BF16)
    return cols(wg), cols(wu), rows


def kernel(x_prompt, x_sample, mem_prompt, state_pool, state_conv, state_h, cache_mem_k, cache_mem_v,
           norm_mix, w_in, pool_w, pool_scale, conv_w, conv_b, gate_a_w, gate_a_b, gate_x_w, gate_x_b,
           rglru_lambda, out_norm_pool, out_norm_rnn, w_out, norm_mem, mem_norm, w_mq, w_mk, w_mv, w_mo,
           norm_ffn, ffn_w_gate, ffn_w_up, ffn_w_down, router_w, moe_w_gate, moe_w_up, moe_w_down,
           final_norm):
    P = dict(norm_mix=norm_mix, w_in=w_in, pool_w=pool_w, pool_scale=pool_scale, conv_w=conv_w,
             conv_b=conv_b, gate_a_w=gate_a_w, gate_a_b=gate_a_b, gate_x_w=gate_x_w,
             gate_x_b=gate_x_b, rglru_lambda=rglru_lambda, out_norm_pool=out_norm_pool,
             out_norm_rnn=out_norm_rnn, w_out=w_out, norm_mem=norm_mem, w_mq=w_mq, w_mo=w_mo,
             norm_ffn=norm_ffn, ffn_w_gate=ffn_w_gate, ffn_w_up=ffn_w_up, ffn_w_down=ffn_w_down,
             router_w=router_w, moe_w_gate=moe_w_gate, moe_w_up=moe_w_up, moe_w_down=moe_w_down)
    bsz, t_len, d = x_prompt.shape
    n_dec = x_sample.shape[0]
    n_tok = bsz * t_len
    fg = final_norm.reshape(1, d)
    TM, TT, TM_DEC, BB = 512, 512, n_dec, SUBLANES

    xp = x_prompt
    xs = x_sample.reshape(n_dec, d)
    mem2 = mem_prompt.reshape(bsz * N_MEM, d)
    mem_ks, mem_vs, pools_p, convs_p, hs_p, pools_s, convs_s, hs_s = ([] for _ in range(8))
    zeros_hist = jnp.zeros((bsz, POOL_PAD, D_POOL), F32)
    zeros_conv = jnp.zeros((bsz, CONV_PAD, D_RNN), F32)
    zeros_h = jnp.zeros((bsz, 1, D_RNN), F32)

    for l in range(DEPTH):
        lw = _layer_weights(l, P)
        last_g = fg if l == DEPTH - 1 else None

        mg = mem_norm[l].reshape(1, d)
        mk = _norm_matmul(mem2, mg, w_mk[l].astype(BF16), TM).reshape(bsz, N_MEM, d)
        mv = _norm_matmul(mem2, mg, w_mv[l].astype(BF16), TM).reshape(bsz, N_MEM, d)
        mem_ks.append(mk.reshape(bsz, N_MEM, MEM_HEADS, MEM_HEAD_DIM))
        mem_vs.append(mv.reshape(bsz, N_MEM, MEM_HEADS, MEM_HEAD_DIM))

        proj = _norm_matmul(xp.reshape(n_tok, d), lw["norm_mix"], lw["w_in"], TM)
        proj = proj.reshape(bsz, t_len, D_IN)
        mixed, h_last = _mixer_prompt(proj, zeros_hist, zeros_conv, zeros_h, lw, 0, TT)
        pools_p.append(proj[:, t_len - POOL_HIST:, :D_POOL])
        convs_p.append(proj[:, t_len - (CONV_WIDTH - 1):, D_POOL:D_MIX])
        hs_p.append(h_last[:, 0, :])
        x2 = _matmul_res(mixed.reshape(n_tok, D_MIX), lw["w_out"], xp.reshape(n_tok, d), TM)
        x3 = _attn_prompt(x2.reshape(bsz, t_len, d), mk, mv, lw["norm_mem"], lw["w_mq"],
                          lw["w_mo"], TT)
        xp = _channel_mixer(x3.reshape(n_tok, d), l, lw, P, TM, last_g, True)
        xp = xp.reshape(bsz, t_len, d)

        proj_s = _norm_matmul(xs, lw["norm_mix"], lw["w_in"], TM_DEC)
        mixed_s, h_new = _mixer_decode(proj_s, jnp.swapaxes(state_pool[l], 0, 1),
                                       jnp.swapaxes(state_conv[l], 0, 1), state_h[l], lw,
                                       PAST_LEN)
        pools_s.append(jnp.concatenate([state_pool[l][:, 1:], proj_s[:, None, :D_POOL]], axis=1))
        convs_s.append(jnp.concatenate([state_conv[l][:, 1:], proj_s[:, None, D_POOL:D_MIX]],
                                       axis=1))
        hs_s.append(h_new)
        xs2 = _matmul_res(mixed_s, lw["w_out"], xs, TM_DEC)
        q_s = _norm_matmul(xs2, lw["norm_mem"], lw["w_mq"], TM_DEC)
        att = _attn_decode(q_s.reshape(n_dec, MEM_HEADS, MEM_HEAD_DIM), cache_mem_k, cache_mem_v,
                           l, BB)
        xs3 = _matmul_res(att.reshape(n_dec, d), lw["w_mo"], xs2, TM_DEC)
        xs = _channel_mixer(xs3, l, lw, P, TM_DEC, last_g, False)

    return (xp, xs.reshape(n_dec, 1, d), jnp.stack(pools_p), jnp.stack(convs_p), jnp.stack(hs_p),
            jnp.stack(mem_ks), jnp.stack(mem_vs), jnp.stack(pools_s), jnp.stack(convs_s),
            jnp.stack(hs_s))
```

```python
import functools
import math

import jax
import jax.numpy as jnp
from jax import lax
from jax.experimental import pallas as pl
from jax.experimental.pallas import tpu as pltpu

F32 = jnp.float32
BF16 = jnp.bfloat16

D_MODEL = 1024
DEPTH = 2
PAST_LEN = 16384
D_POOL = 512
POOL_WINDOWS = (2, 4, 8, 16)
POOL_GROUP = 128
POOL_HIST = 15
D_RNN = 512
RNN_HEADS = 8
CONV_WIDTH = 4
RGLRU_C = 8.0
D_MIX = 1024
D_IN = 1536
N_MEM = 256
MEM_HEADS = 4
MEM_HEAD_DIM = 256
N_EXPERTS = 8
D_FF_CHUNK = 1408
EPS = 1e-6

SUBLANES = 8
LANES = 128
POOL_PAD = 16
CONV_PAD = 8
VMEM_LIMIT = 56 * 1024 * 1024
SLOT_MAP_CHUNK = 2048
TOK_BITS = 14
MOE_CHUNK = 256
MOE_TILE = 576


def _cparams(sem):
    return pltpu.CompilerParams(dimension_semantics=sem, vmem_limit_bytes=VMEM_LIMIT)


def _rms(x, g):
    ms = jnp.mean(x * x, axis=-1, keepdims=True)
    return x * lax.rsqrt(ms + EPS) * g


def _gelu_tanh(x):
    c = math.sqrt(2.0 / math.pi)
    return 0.5 * x * (1.0 + jnp.tanh(c * (x + 0.044715 * (x * x * x))))


def _softplus(x):
    return jnp.maximum(x, 0.0) + jnp.log1p(jnp.exp(-jnp.abs(x)))


def _norm_matmul_body(x_ref, g_ref, w_ref, o_ref):
    hn = _rms(x_ref[...], g_ref[...]).astype(BF16)
    o_ref[...] = jnp.dot(hn, w_ref[...], preferred_element_type=F32)


def _norm_matmul(x, g, w, tm):
    n, d = x.shape
    m = w.shape[1]
    return pl.pallas_call(
        _norm_matmul_body,
        grid=(n // tm,),
        in_specs=[pl.BlockSpec((tm, d), lambda i: (i, 0)),
                  pl.BlockSpec((1, d), lambda i: (0, 0)),
                  pl.BlockSpec((d, m), lambda i: (0, 0))],
        out_specs=pl.BlockSpec((tm, m), lambda i: (i, 0)),
        out_shape=jax.ShapeDtypeStruct((n, m), F32),
        compiler_params=_cparams(("parallel",)),
        name="norm_matmul",
    )(x, g, w)


def _matmul_res_body(a_ref, w_ref, r_ref, o_ref):
    o_ref[...] = r_ref[...] + jnp.dot(a_ref[...].astype(BF16), w_ref[...],
                                      preferred_element_type=F32)


def _matmul_res(a, w, res, tm):
    n, k = a.shape
    m = w.shape[1]
    return pl.pallas_call(
        _matmul_res_body,
        grid=(n // tm,),
        in_specs=[pl.BlockSpec((tm, k), lambda i: (i, 0)),
                  pl.BlockSpec((k, m), lambda i: (0, 0)),
                  pl.BlockSpec((tm, m), lambda i: (i, 0))],
        out_specs=pl.BlockSpec((tm, m), lambda i: (i, 0)),
        out_shape=jax.ShapeDtypeStruct((n, m), F32),
        compiler_params=_cparams(("parallel",)),
        name="matmul_res",
    )(a, w, res)


def _pool_project(z_groups, pw_ref, scale):
    ys = [jnp.dot(z.astype(BF16), pw_ref[g], preferred_element_type=F32)
          for g, z in enumerate(z_groups)]
    return jnp.concatenate(ys, axis=-1) * scale


def _rglru_terms(c, first_pos, wa_ref, ba_ref, wx_ref, bx_ref, lam_ref):
    cb = c.astype(BF16)
    r = jax.nn.sigmoid(jnp.dot(cb, wa_ref[...], preferred_element_type=F32) + ba_ref[...])
    i = jax.nn.sigmoid(jnp.dot(cb, wx_ref[...], preferred_element_type=F32) + bx_ref[...])
    log_a = (-RGLRU_C) * r * _softplus(-lam_ref[...])
    a = jnp.exp(log_a)
    mult = jnp.sqrt(1.0 - a * a)
    if first_pos is not None:
        mult = jnp.where(first_pos, 1.0, mult)
    return a, mult * (i * c)


def _mix_out(y_pool, h, u_gate, gp_ref, gr_ref):
    y_rnn = h * _gelu_tanh(u_gate)
    return jnp.concatenate([_rms(y_pool, gp_ref[...]), _rms(y_rnn, gr_ref[...])], axis=-1)


def _mixer_body(pos0, tt, proj_ref, hist_ref, conv_ref, h0_ref, pw_ref, ps_ref, cw_ref, cb_ref,
                wa_ref, ba_ref, wx_ref, bx_ref, lam_ref, gp_ref, gr_ref,
                mix_ref, hl_ref, pbuf, cbuf, a_s, b_s, hc):
    t = pl.program_id(1)

    @pl.when(t == 0)
    def _():
        pbuf[0:POOL_PAD, :] = hist_ref[...]
        cbuf[0:CONV_PAD, :] = conv_ref[...]
        hc[...] = jnp.broadcast_to(h0_ref[...], (SUBLANES, D_RNN))

    @pl.when(t > 0)
    def _():
        pbuf[0:POOL_PAD, :] = pbuf[tt:tt + POOL_PAD, :]
        cbuf[0:CONV_PAD, :] = cbuf[tt:tt + CONV_PAD, :]

    u_pool = proj_ref[:, 0:D_POOL]
    pbuf[POOL_PAD:POOL_PAD + tt, :] = u_pool
    cbuf[CONV_PAD:CONV_PAD + tt, :] = proj_ref[:, D_POOL:D_MIX]

    pos = pos0 + t * tt + lax.broadcasted_iota(jnp.int32, (tt, 1), 0)

    zs = []
    for g, w in enumerate(POOL_WINDOWS):
        sl = slice(g * POOL_GROUP, (g + 1) * POOL_GROUP)
        acc = pbuf[POOL_PAD:POOL_PAD + tt, sl]
        for k in range(1, w):
            acc = acc + pbuf[POOL_PAD - k:POOL_PAD - k + tt, sl]
        cnt = jnp.minimum(pos + 1, w).astype(F32)
        zs.append(acc / cnt - u_pool[:, sl])
    y_pool = _pool_project(zs, pw_ref, ps_ref[...])

    c = cb_ref[...]
    for k in range(CONV_WIDTH):
        off = CONV_PAD - (CONV_WIDTH - 1) + k
        c = c + cbuf[off:off + tt, :] * cw_ref[k:k + 1, :]

    a, b = _rglru_terms(c, pos == 0, wa_ref, ba_ref, wx_ref, bx_ref, lam_ref)
    a_s[...] = a
    b_s[...] = b

    row = lax.broadcasted_iota(jnp.int32, (SUBLANES, D_RNN), 0)

    def group(j, carry):
        r0 = pl.multiple_of(j * SUBLANES, SUBLANES)
        ga = a_s[pl.ds(r0, SUBLANES), :]
        gb = b_s[pl.ds(r0, SUBLANES), :]
        for s in (1, 2, 4):
            keep = row >= s
            gb = jnp.where(keep, ga * pltpu.roll(gb, s, axis=0) + gb, gb)
            ga = jnp.where(keep, ga * pltpu.roll(ga, s, axis=0), ga)
        h = ga * carry + gb
        b_s[pl.ds(r0, SUBLANES), :] = h
        return jnp.broadcast_to(h[SUBLANES - 1:SUBLANES, :], (SUBLANES, D_RNN))

    carry = lax.fori_loop(0, tt // SUBLANES, group, hc[...])
    hc[...] = carry
    hl_ref[...] = carry

    mix = _mix_out(y_pool, b_s[...], proj_ref[:, D_MIX:D_IN], gp_ref, gr_ref)
    mix_ref[...] = mix.astype(BF16)


def _mixer_prompt(proj, hist, conv, h0, lw, pos0, tt):
    bsz, t_len, _ = proj.shape
    full = lambda shape: pl.BlockSpec(shape, lambda b, t: (0,) * len(shape))
    per_b = lambda rows, cols: pl.BlockSpec((None, rows, cols), lambda b, t: (b, 0, 0))
    return pl.pallas_call(
        functools.partial(_mixer_body, pos0, tt),
        grid=(bsz, t_len // tt),
        in_specs=[pl.BlockSpec((None, tt, D_IN), lambda b, t: (b, t, 0)),
                  per_b(POOL_PAD, D_POOL), per_b(CONV_PAD, D_RNN), per_b(1, D_RNN),
                  full((len(POOL_WINDOWS), POOL_GROUP, POOL_GROUP)), full((1, D_POOL)),
                  full((CONV_WIDTH, D_RNN)), full((1, D_RNN)),
                  full((D_RNN, D_RNN)), full((1, D_RNN)),
                  full((D_RNN, D_RNN)), full((1, D_RNN)),
                  full((1, D_RNN)), full((1, D_POOL)), full((1, D_RNN))],
        out_specs=[pl.BlockSpec((None, tt, D_MIX), lambda b, t: (b, t, 0)),
                   per_b(SUBLANES, D_RNN)],
        out_shape=[jax.ShapeDtypeStruct((bsz, t_len, D_MIX), BF16),
                   jax.ShapeDtypeStruct((bsz, SUBLANES, D_RNN), F32)],
        scratch_shapes=[pltpu.VMEM((POOL_PAD + tt, D_POOL), F32),
                        pltpu.VMEM((CONV_PAD + tt, D_RNN), F32),
                        pltpu.VMEM((tt, D_RNN), F32),
                        pltpu.VMEM((tt, D_RNN), F32),
                        pltpu.VMEM((SUBLANES, D_RNN), F32)],
        compiler_params=_cparams(("parallel", "arbitrary")),
        name="mixer_prompt",
    )(proj, hist, conv, h0, lw["pool_w"], lw["pool_scale"], lw["conv_w"], lw["conv_b"],
      lw["gate_a_w"], lw["gate_a_b"], lw["gate_x_w"], lw["gate_x_b"], lw["lam"],
      lw["out_norm_pool"], lw["out_norm_rnn"])


def _mixer_decode_body(pos0, proj_ref, hist_ref, conv_ref, h0_ref, pw_ref, ps_ref, cw_ref, cb_ref,
                       wa_ref, ba_ref, wx_ref, bx_ref, lam_ref, gp_ref, gr_ref, mix_ref, hn_ref):
    u_pool = proj_ref[:, 0:D_POOL]
    zs = []
    for g, w in enumerate(POOL_WINDOWS):
        sl = slice(g * POOL_GROUP, (g + 1) * POOL_GROUP)
        acc = u_pool[:, sl]
        for k in range(1, w):
            acc = acc + hist_ref[POOL_HIST - k, :, sl]
        zs.append(acc / float(min(pos0 + 1, w)) - u_pool[:, sl])
    y_pool = _pool_project(zs, pw_ref, ps_ref[...])

    c = cb_ref[...]
    for k in range(CONV_WIDTH - 1):
        c = c + conv_ref[k] * cw_ref[k:k + 1, :]
    c = c + proj_ref[:, D_POOL:D_MIX] * cw_ref[CONV_WIDTH - 1:CONV_WIDTH, :]

    a, b = _rglru_terms(c, None, wa_ref, ba_ref, wx_ref, bx_ref, lam_ref)
    h = a * h0_ref[...] + b
    hn_ref[...] = h
    mix_ref[...] = _mix_out(y_pool, h, proj_ref[:, D_MIX:D_IN], gp_ref, gr_ref).astype(BF16)


def _mixer_decode(proj, hist_t, conv_t, h0, lw, pos0):
    assert pos0 > 0
    n = proj.shape[0]
    vm = pl.BlockSpec(memory_space=pltpu.VMEM)
    return pl.pallas_call(
        functools.partial(_mixer_decode_body, pos0),
        in_specs=[vm] * 15,
        out_specs=[vm, vm],
        out_shape=[jax.ShapeDtypeStruct((n, D_MIX), BF16),
                   jax.ShapeDtypeStruct((n, D_RNN), F32)],
        compiler_params=pltpu.CompilerParams(vmem_limit_bytes=VMEM_LIMIT),
        name="mixer_decode",
    )(proj, hist_t, conv_t, h0, lw["pool_w"], lw["pool_scale"], lw["conv_w"], lw["conv_b"],
      lw["gate_a_w"], lw["gate_a_b"], lw["gate_x_w"], lw["gate_x_b"], lw["lam"],
      lw["out_norm_pool"], lw["out_norm_rnn"])


def _softmax_rows(s):
    e = jnp.exp(s - jnp.max(s, axis=-1, keepdims=True))
    return e / jnp.sum(e, axis=-1, keepdims=True)


_NT = (((1,), (1,)), ((), ()))


def _attn_prompt_body(x_ref, k_ref, v_ref, g_ref, wq_ref, wo_ref, o_ref):
    x = x_ref[...]
    xn = _rms(x, g_ref[...]).astype(BF16)
    q = jnp.dot(xn, wq_ref[...], preferred_element_type=F32)
    outs = []
    for h in range(MEM_HEADS):
        sl = slice(h * MEM_HEAD_DIM, (h + 1) * MEM_HEAD_DIM)
        s = lax.dot_general(q[:, sl].astype(BF16), k_ref[:, sl].astype(BF16), _NT,
                            preferred_element_type=F32) * (MEM_HEAD_DIM ** -0.5)
        p = _softmax_rows(s).astype(BF16)
        outs.append(jnp.dot(p, v_ref[:, sl].astype(BF16), preferred_element_type=F32))
    o = jnp.concatenate(outs, axis=-1).astype(BF16)
    o_ref[...] = x + jnp.dot(o, wo_ref[...], preferred_element_type=F32)


def _attn_prompt(x, k, v, g, wq, wo, tt):
    bsz, t_len, d = x.shape
    full = lambda shape: pl.BlockSpec(shape, lambda b, t: (0,) * len(shape))
    kv = pl.BlockSpec((None, N_MEM, d), lambda b, t: (b, 0, 0))
    xs = pl.BlockSpec((None, tt, d), lambda b, t: (b, t, 0))
    return pl.pallas_call(
        _attn_prompt_body,
        grid=(bsz, t_len // tt),
        in_specs=[xs, kv, kv, full((1, d)), full((d, d)), full((d, d))],
        out_specs=xs,
        out_shape=jax.ShapeDtypeStruct(x.shape, F32),
        compiler_params=_cparams(("parallel", "parallel")),
        name="attn_prompt",
    )(x, k, v, g, wq, wo)


def _attn_decode_body(bb, q_ref, k_ref, v_ref, o_ref):
    for b in range(bb):
        q = q_ref[b]
        s = jnp.sum(k_ref[b] * q[None], axis=-1, keepdims=True) * (MEM_HEAD_DIM ** -0.5)
        e = jnp.exp(s - jnp.max(s, axis=0, keepdims=True))
        p = e / jnp.sum(e, axis=0, keepdims=True)
        o_ref[b] = jnp.sum(p * v_ref[b], axis=0)


def _attn_decode(q, k, v, layer, bb):
    n, nh, hd = q.shape
    kv = pl.BlockSpec((None, bb, N_MEM, nh, hd), lambda i: (layer, i, 0, 0, 0))
    qs = pl.BlockSpec((bb, nh, hd), lambda i: (i, 0, 0))
    return pl.pallas_call(
        functools.partial(_attn_decode_body, bb),
        grid=(n // bb,),
        in_specs=[qs, kv, kv],
        out_specs=qs,
        out_shape=jax.ShapeDtypeStruct((n, nh, hd), F32),
        compiler_params=_cparams(("parallel",)),
        name="attn_decode",
    )(q, k, v)


def _top2(hn, rw_ref):
    logits = jnp.dot(hn, rw_ref[...], precision=lax.Precision.HIGHEST,
                     preferred_element_type=F32)
    lane = lax.broadcasted_iota(jnp.int32, logits.shape, 1)
    neg = -jnp.inf
    lg = jnp.where(lane < N_EXPERTS, logits, neg)
    m1 = jnp.max(lg, axis=-1, keepdims=True)
    i1 = jnp.min(jnp.where(lg == m1, lane, LANES), axis=-1, keepdims=True)
    lg2 = jnp.where(lane == i1, neg, lg)
    m2 = jnp.max(lg2, axis=-1, keepdims=True)
    i2 = jnp.min(jnp.where(lg2 == m2, lane, LANES), axis=-1, keepdims=True)
    e2 = jnp.exp(m2 - m1)
    den = 1.0 + e2
    return lane, i1, i2, 1.0 / den, e2 / den


def _router_body(x_ref, g_ref, rw_ref, cw_ref):
    lane, i1, i2, g1, g2 = _top2(_rms(x_ref[...], g_ref[...]), rw_ref)
    cw_ref[...] = jnp.where(lane == i1, g1, 0.0) + jnp.where(lane == i2, g2, 0.0)


def _router(x, g, rw, tm):
    n, d = x.shape
    return pl.pallas_call(
        _router_body,
        grid=(n // tm,),
        in_specs=[pl.BlockSpec((tm, d), lambda i: (i, 0)),
                  pl.BlockSpec((1, d), lambda i: (0, 0)),
                  pl.BlockSpec((d, LANES), lambda i: (0, 0))],
        out_specs=pl.BlockSpec((tm, LANES), lambda i: (i, 0)),
        out_shape=jax.ShapeDtypeStruct((n, LANES), F32),
        compiler_params=_cparams(("parallel",)),
        name="router",
    )(x, g, rw)


def _store_row_tiles(ref, val):
    rows = val.shape[0]
    for c in range(val.shape[1] // LANES):
        ref[pl.ds(c, rows, stride=SUBLANES), :] = val[:, c * LANES:(c + 1) * LANES]


def _load_row_tiles(ref, rows):
    return jnp.concatenate([ref[pl.ds(c, rows, stride=SUBLANES), :] for c in range(SUBLANES)],
                           axis=-1)


R_E1, R_E2, R_RANK1, R_RANK2, R_G1, R_G2 = range(6)


def _router_rank_body(x_ref, g_ref, rw_ref, rec_ref, rows_ref, cnt_ref, hn_ref, carry):
    i = pl.program_id(0)
    tm = x_ref.shape[0]

    @pl.when(i == 0)
    def _():
        carry[...] = jnp.zeros_like(carry)

    hn = _rms(x_ref[...], g_ref[...])
    _store_row_tiles(hn_ref, hn)
    lane, i1, i2, g1, g2 = _top2(hn, rw_ref)
    picks = ((lane == i1) | (lane == i2)).astype(F32)
    r_io = lax.broadcasted_iota(jnp.int32, (tm, tm), 0)
    c_io = lax.broadcasted_iota(jnp.int32, (tm, tm), 1)
    earlier = (c_io < r_io).astype(BF16)
    before = jnp.dot(earlier, picks.astype(BF16), preferred_element_type=F32) + carry[...]
    rank1 = jnp.sum(jnp.where(lane == i1, before, 0.0), axis=-1, keepdims=True)
    rank2 = jnp.sum(jnp.where(lane == i2, before, 0.0), axis=-1, keepdims=True)
    carry[...] += jnp.sum(picks, axis=0, keepdims=True)
    cnt_ref[...] = jnp.broadcast_to(carry[...], cnt_ref.shape)

    rec = jnp.zeros((tm, LANES), F32)
    for k, v in ((R_E1, i1.astype(F32)), (R_E2, i2.astype(F32)), (R_RANK1, rank1),
                 (R_RANK2, rank2), (R_G1, g1), (R_G2, g2)):
        rec = jnp.where(lane == k, v, rec)
    rec_ref[...] = rec
    rows_ref[...] = rec.T[0:SUBLANES, :].astype(jnp.int32)


def _router_rank(x, g, rw, tm):
    n, d = x.shape
    assert d == SUBLANES * LANES
    return pl.pallas_call(
        _router_rank_body,
        grid=(n // tm,),
        in_specs=[pl.BlockSpec((tm, d), lambda i: (i, 0)),
                  pl.BlockSpec((1, d), lambda i: (0, 0)),
                  pl.BlockSpec((d, LANES), lambda i: (0, 0))],
        out_specs=[pl.BlockSpec((tm, LANES), lambda i: (i, 0)),
                   pl.BlockSpec((SUBLANES, tm), lambda i: (0, i)),
                   pl.BlockSpec((SUBLANES, LANES), lambda i: (0, 0)),
                   pl.BlockSpec((tm * SUBLANES, LANES), lambda i: (i, 0))],
        out_shape=[jax.ShapeDtypeStruct((n, LANES), F32),
                   jax.ShapeDtypeStruct((SUBLANES, n), jnp.int32),
                   jax.ShapeDtypeStruct((SUBLANES, LANES), F32),
                   jax.ShapeDtypeStruct((n * SUBLANES, LANES), F32)],
        scratch_shapes=[pltpu.VMEM((1, LANES), F32)],
        compiler_params=_cparams(("arbitrary",)),
        name="router_rank",
    )(x, g, rw)


def _slot_map_body(n_tok, tm, n_tiles, cnt_ref, rows_ref, slot_ref, te_ref, nu_ref, off_s,
                   pos_v, pos1_s, pos2_s):
    c = pl.program_id(0)
    tc = rows_ref.shape[1]

    def fill_pad(s, ordinal):
        slot_ref[s] = (2 * n_tok + ordinal) << TOK_BITS
        return ordinal + 1

    @pl.when(c == 0)
    def _():
        lead = lax.fori_loop(0, tm, fill_pad, jnp.int32(0))

        def per_expert(e, carry):
            tile0, ordinal = carry
            n = cnt_ref[e]
            start = (tile0 + 1) * tm
            off_s[e] = start
            tile1 = tile0 + lax.div(n + (tm - 1), tm)

            def fill_tile(t, z):
                te_ref[t] = e
                return z

            lax.fori_loop(tile0, tile1, fill_tile, 0)
            ordinal = lax.fori_loop(start + n, (tile1 + 1) * tm, fill_pad, ordinal)
            return tile1, ordinal

        used, ordinal = lax.fori_loop(0, N_EXPERTS, per_expert, (jnp.int32(0), lead))
        nu_ref[0] = used

        def fill_rest(t, z):
            te_ref[t] = N_EXPERTS - 1
            return z

        lax.fori_loop(used, n_tiles, fill_rest, 0)
        lax.fori_loop((used + 1) * tm, (n_tiles + 2) * tm, fill_pad, ordinal)

    rows = rows_ref[...]
    pos = rows[R_RANK1:R_RANK2 + 1, :]
    for e in range(N_EXPERTS):
        pos = pos + jnp.where(rows[R_E1:R_E2 + 1, :] == e, off_s[e], 0)
    pos_v[...] = pos
    pltpu.sync_copy(pos_v.at[0], pos1_s)
    pltpu.sync_copy(pos_v.at[1], pos2_s)

    step = (1 << TOK_BITS) + 1

    def per_token(j, first):
        slot_ref[pos1_s[j]] = first
        slot_ref[pos2_s[j]] = first + (n_tok << TOK_BITS)
        return first + step

    lax.fori_loop(0, tc, per_token, (c * tc) * step, unroll=8)


def _slot_map(counts, rows, n_tok, tm, n_tiles, tc):
    smem = pl.BlockSpec(memory_space=pltpu.SMEM)
    return pl.pallas_call(
        functools.partial(_slot_map_body, n_tok, tm, n_tiles),
        grid=(n_tok // tc,),
        in_specs=[smem, pl.BlockSpec((SUBLANES, tc), lambda c: (0, c))],
        out_specs=[smem, smem, smem],
        out_shape=[jax.ShapeDtypeStruct(((n_tiles + 2) * tm,), jnp.int32),
                   jax.ShapeDtypeStruct((n_tiles,), jnp.int32),
                   jax.ShapeDtypeStruct((1,), jnp.int32)],
        scratch_shapes=[pltpu.SMEM((N_EXPERTS,), jnp.int32), pltpu.VMEM((2, tc), jnp.int32),
                        pltpu.SMEM((tc,), jnp.int32), pltpu.SMEM((tc,), jnp.int32)],
        compiler_params=_cparams(("arbitrary",)),
        name="slot_map",
    )(counts, rows)


def _experts_body(tm, slot_ref, te_ref, nu_ref, x_hbm, wg_ref, wu_ref, wd_ref,
                  out_hbm, xbuf, ybuf, hn_s, mid_s, acc, gsem, ssem, zsem):
    del te_ref
    i = pl.program_id(0)
    n_used = nu_ref[0]
    cur = i % 2
    nxt = 1 - cur

    def tile_rows(r):
        return pl.ds(pl.multiple_of(r * SUBLANES, SUBLANES), SUBLANES)

    def gather_copy(src_row, r, buf):
        return pltpu.make_async_copy(x_hbm.at[tile_rows(src_row), :],
                                     xbuf.at[buf, tile_rows(r), :], gsem.at[buf])

    def scatter_copy(dst_row, r, buf):
        return pltpu.make_async_copy(ybuf.at[buf, tile_rows(r), :],
                                     out_hbm.at[tile_rows(dst_row), :], ssem.at[buf])

    def gather_src(tile, r):
        return slot_ref[(tile + 1) * tm + r] & ((1 << TOK_BITS) - 1)

    def scatter_dst(tile, r):
        return slot_ref[(tile + 1) * tm + r] >> TOK_BITS

    def wait_gathers(buf):
        for r in range(tm):
            gather_copy(0, r, buf).wait()

    def wait_scatters(buf):
        for r in range(tm):
            scatter_copy(0, r, buf).wait()

    @pl.when(i == 0)
    def _():
        ybuf[...] = jnp.zeros_like(ybuf)

        def zero_tile(j, z):
            rows = pl.ds(pl.multiple_of(j * (tm * SUBLANES), SUBLANES), tm * SUBLANES)
            cp = pltpu.make_async_copy(ybuf.at[0], out_hbm.at[rows, :], zsem.at[0])
            cp.start()
            cp.wait()
            return z

        lax.fori_loop(n_used + 1, out_hbm.shape[0] // (tm * SUBLANES), zero_tile, 0)
        for r in range(tm):
            gather_copy(gather_src(0, r), r, 0).start()

    @pl.when(i < n_used)
    def _():
        wait_gathers(cur)
        hn_s[...] = _load_row_tiles(xbuf.at[cur], tm).astype(BF16)
        acc[...] = jnp.zeros_like(acc)
        mid_s[...] = jnp.zeros_like(mid_s)
        n_chunks = wg_ref.shape[0]
        half = n_chunks // 2
        rows_per = tm // half

        def chunk(start_rows, k, carry):
            start_rows(pl.multiple_of((k % half) * rows_per, SUBLANES))
            acc[...] += jnp.dot(mid_s[...], wd_ref[jnp.maximum(k - 1, 0)],
                                preferred_element_type=F32)
            hn = hn_s[...]
            gate = jnp.dot(hn, wg_ref[k], preferred_element_type=F32)
            up = jnp.dot(hn, wu_ref[k], preferred_element_type=F32)
            mid_s[...] = (gate * jax.nn.sigmoid(gate) * up).astype(BF16)
            return carry

        def start_gather_rows(row0):
            base = (i + 2) * tm + row0
            for r in range(rows_per):
                src = slot_ref[base + r] & ((1 << TOK_BITS) - 1)
                gather_copy(src, row0 + r, nxt).start()

        def start_scatter_rows(row0):
            base = i * tm + row0
            for r in range(rows_per):
                scatter_copy(slot_ref[base + r] >> TOK_BITS, row0 + r, nxt).start()

        lax.fori_loop(0, half, functools.partial(chunk, start_gather_rows), 0)
        lax.fori_loop(half, n_chunks, functools.partial(chunk, start_scatter_rows), 0)
        y = acc[...] + jnp.dot(mid_s[...], wd_ref[n_chunks - 1], preferred_element_type=F32)

        @pl.when(i >= 1)
        def _():
            wait_scatters(cur)

        _store_row_tiles(ybuf.at[cur], y)

    @pl.when(i == n_used)
    def _():
        wait_gathers(cur)
        wait_scatters(cur)
        for r in range(tm):
            scatter_copy(scatter_dst(i - 1, r), r, nxt).start()
        wait_scatters(nxt)


def _experts_sparse(xt, wg, wu, wd, slot, tile_expert, n_used, tm, n_tiles):
    n_tok = xt.shape[0] // SUBLANES
    _, n_chunks, d, fc = wg.shape
    n_pad = (n_tiles + 2) * tm - 2 * n_tok
    assert n_tok <= 1 << TOK_BITS and 2 * n_tok + n_pad <= 1 << (31 - TOK_BITS)
    assert n_chunks % 2 == 0 and tm % ((n_chunks // 2) * SUBLANES) == 0
    w_idx = lambda i, slot, te, nu: (te[jnp.minimum(i, n_tiles - 1)], 0, 0, 0)
    grid_spec = pltpu.PrefetchScalarGridSpec(
        num_scalar_prefetch=3,
        grid=(n_tiles + 1,),
        in_specs=[pl.BlockSpec(memory_space=pl.ANY),
                  pl.BlockSpec((None, n_chunks, d, fc), w_idx),
                  pl.BlockSpec((None, n_chunks, d, fc), w_idx),
                  pl.BlockSpec((None, n_chunks, fc, d), w_idx)],
        out_specs=pl.BlockSpec(memory_space=pl.ANY),
        scratch_shapes=[pltpu.VMEM((2, tm * SUBLANES, LANES), F32),
                        pltpu.VMEM((2, tm * SUBLANES, LANES), F32),
                        pltpu.VMEM((tm, d), BF16), pltpu.VMEM((tm, fc), BF16),
                        pltpu.VMEM((tm, d), F32),
                        pltpu.SemaphoreType.DMA((2,)), pltpu.SemaphoreType.DMA((2,)),
                        pltpu.SemaphoreType.DMA((1,))])
    return pl.pallas_call(
        functools.partial(_experts_body, tm),
        grid_spec=grid_spec,
        out_shape=jax.ShapeDtypeStruct(((2 * n_tok + n_pad) * SUBLANES, LANES), F32),
        compiler_params=_cparams(("arbitrary",)),
        name="experts_sparse",
    )(slot, tile_expert, n_used, xt, wg, wu, wd)


def _combine_body(final, x_ref, y1_ref, y2_ref, rec_ref, *rest):
    o_ref = rest[-1]
    tm = x_ref.shape[0]
    rec = rec_ref[...]
    out = (x_ref[...] + rec[:, R_G1:R_G1 + 1] * _load_row_tiles(y1_ref, tm)
           + rec[:, R_G2:R_G2 + 1] * _load_row_tiles(y2_ref, tm))
    o_ref[...] = _rms(out, rest[0][...]) if final else out


def _combine(x, y, rec, tm, final_g=None):
    n, d = x.shape
    final = final_g is not None
    nb = n // tm
    row = pl.BlockSpec((tm, d), lambda i: (i, 0))
    in_specs = [row, pl.BlockSpec((tm * SUBLANES, LANES), lambda i: (i, 0)),
                pl.BlockSpec((tm * SUBLANES, LANES), lambda i: (nb + i, 0)),
                pl.BlockSpec((tm, LANES), lambda i: (i, 0))]
    args = [x, y, y, rec]
    if final:
        in_specs.append(pl.BlockSpec((1, d), lambda i: (0, 0)))
        args.append(final_g)
    return pl.pallas_call(
        functools.partial(_combine_body, final),
        grid=(n // tm,),
        in_specs=in_specs,
        out_specs=row,
        out_shape=jax.ShapeDtypeStruct((n, d), F32),
        compiler_params=_cparams(("parallel",)),
        name="moe_combine",
    )(*args)


def _ffn_body(n_chunks, per_expert, final, *refs):
    routed = per_expert is not None
    x_ref, g_ref, wg_ref, wu_ref, wd_ref = refs[:5]
    rest = list(refs[5:])
    cw_ref = rest.pop(0) if routed else None
    fg_ref = rest.pop(0) if final else None
    o_ref, hn_s = rest
    j = pl.program_id(1)

    @pl.when(j == 0)
    def _():
        hn_s[...] = _rms(x_ref[...], g_ref[...]).astype(BF16)
        o_ref[...] = x_ref[...]

    hn = hn_s[...]
    gate = jnp.dot(hn, wg_ref[...], preferred_element_type=F32)
    up = jnp.dot(hn, wu_ref[...], preferred_element_type=F32)
    mid = (gate * jax.nn.sigmoid(gate) * up).astype(BF16)
    y = jnp.dot(mid, wd_ref[...], preferred_element_type=F32)
    if routed:
        cw = cw_ref[...]
        lane = lax.broadcasted_iota(jnp.int32, cw.shape, 1)
        y = y * jnp.sum(jnp.where(lane == j // per_expert, cw, 0.0), axis=-1, keepdims=True)
    o_ref[...] += y

    if final:
        @pl.when(j == n_chunks - 1)
        def _():
            o_ref[...] = _rms(o_ref[...], fg_ref[...])


def _ffn(x, g, wg, wu, wd, tm, cw=None, final_g=None):
    n, d = x.shape
    routed = cw is not None
    final = final_g is not None
    if routed:
        n_e, per_expert, _, fc = wg.shape
        n_chunks = n_e * per_expert
        w_idx = lambda i, j: (j // per_expert, j % per_expert, 0, 0)
        w_in_spec = pl.BlockSpec((None, None, d, fc), w_idx)
        w_dn_spec = pl.BlockSpec((None, None, fc, d), w_idx)
    else:
        fc = D_FF_CHUNK
        per_expert = None
        n_chunks = wg.shape[1] // fc
        w_in_spec = pl.BlockSpec((d, fc), lambda i, j: (0, j))
        w_dn_spec = pl.BlockSpec((fc, d), lambda i, j: (j, 0))
    row = pl.BlockSpec((tm, d), lambda i, j: (i, 0))
    vec = pl.BlockSpec((1, d), lambda i, j: (0, 0))
    in_specs = [row, vec, w_in_spec, w_in_spec, w_dn_spec]
    args = [x, g, wg, wu, wd]
    if routed:
        in_specs.append(pl.BlockSpec((tm, LANES), lambda i, j: (i, 0)))
        args.append(cw)
    if final:
        in_specs.append(vec)
        args.append(final_g)
    return pl.pallas_call(
        functools.partial(_ffn_body, n_chunks, per_expert, final),
        grid=(n // tm, n_chunks),
        in_specs=in_specs,
        out_specs=row,
        out_shape=jax.ShapeDtypeStruct((n, d), F32),
        scratch_shapes=[pltpu.VMEM((tm, d), BF16)],
        compiler_params=_cparams(("parallel", "arbitrary")),
        name="ffn",
    )(*args)


def _block_diag(w):
    h, a, b = w.shape
    eye = jnp.eye(h, dtype=w.dtype)
    return (eye[:, None, :, None] * w[:, :, None, :]).reshape(h * a, h * b)


def _layer_weights(l, P):
    row = lambda v: v[l].reshape(1, -1)
    return dict(
        norm_mix=row(P["norm_mix"]), w_in=P["w_in"][l].astype(BF16),
        pool_w=P["pool_w"][l].astype(BF16), pool_scale=row(P["pool_scale"]),
        conv_w=P["conv_w"][l], conv_b=row(P["conv_b"]),
        gate_a_w=_block_diag(P["gate_a_w"][l]).astype(BF16), gate_a_b=row(P["gate_a_b"]),
        gate_x_w=_block_diag(P["gate_x_w"][l]).astype(BF16), gate_x_b=row(P["gate_x_b"]),
        lam=row(P["rglru_lambda"]),
        out_norm_pool=row(P["out_norm_pool"]), out_norm_rnn=row(P["out_norm_rnn"]),
        w_out=P["w_out"][l].astype(BF16), norm_mem=row(P["norm_mem"]),
        w_mq=P["w_mq"][l].astype(BF16), w_mo=P["w_mo"][l].astype(BF16),
        norm_ffn=row(P["norm_ffn"]),
        moe=(_chunk_experts(P["moe_w_gate"][l // 2], P["moe_w_up"][l // 2], P["moe_w_down"][l // 2])
             if l % 2 == 1 else None))


def _channel_mixer(x, l, lw, P, tm, final_g, sparse):
    j = l // 2
    n = x.shape[0]
    if l % 2 == 0:
        return _ffn(x, lw["norm_ffn"], P["ffn_w_gate"][j].astype(BF16),
                    P["ffn_w_up"][j].astype(BF16), P["ffn_w_down"][j].astype(BF16),
                    tm, final_g=final_g)
    rw = jnp.pad(P["router_w"][j], ((0, 0), (0, LANES - N_EXPERTS)))
    wg, wu, wd = lw["moe"]
    if not sparse:
        cw = _router(x, lw["norm_ffn"], rw, tm)
        return _ffn(x, lw["norm_ffn"], wg, wu, wd, tm, cw=cw, final_g=final_g)
    te = MOE_TILE
    n_tiles = -(-2 * n // te) + N_EXPERTS
    rec, rows, cnt, xt = _router_rank(x, lw["norm_ffn"], rw, tm)
    counts = cnt[0, :N_EXPERTS].astype(jnp.int32)
    slot, tile_expert, n_used = _slot_map(counts, rows, n, te, n_tiles, SLOT_MAP_CHUNK)
    y = _experts_sparse(xt, wg, wu, wd, slot, tile_expert, n_used, te, n_tiles)
    return _combine(x, y, rec, tm, final_g=final_g)


def _chunk_experts(wg, wu, wd):
    e, d, f = wg.shape
    c = -(-f // MOE_CHUNK)
    pad = c * MOE_CHUNK - f

    def cols(w):
        w = jnp.pad(w, ((0, 0), (0, 0), (0, pad))).reshape(e, d, c, MOE_CHUNK)
        return jnp.transpose(w, (0, 2, 1, 3)).astype(BF16)

    rows = jnp.pad(wd, ((0, 0), (0, pad), (0, 0))).reshape(e, c, MOE_CHUNK, d).astype(BF16)
    return cols(wg), cols(wu), rows


def kernel(x_prompt, x_sample, mem_prompt, state_pool, state_conv, state_h, cache_mem_k, cache_mem_v,
           norm_mix, w_in, pool_w, pool_scale, conv_w, conv_b, gate_a_w, gate_a_b, gate_x_w, gate_x_b,
           rglru_lambda, out_norm_pool, out_norm_rnn, w_out, norm_mem, mem_norm, w_mq, w_mk, w_mv, w_mo,
           norm_ffn, ffn_w_gate, ffn_w_up, ffn_w_down, router_w, moe_w_gate, moe_w_up, moe_w_down,
           final_norm):
    P = dict(norm_mix=norm_mix, w_in=w_in, pool_w=pool_w, pool_scale=pool_scale, conv_w=conv_w,
             conv_b=conv_b, gate_a_w=gate_a_w, gate_a_b=gate_a_b, gate_x_w=gate_x_w,
             gate_x_b=gate_x_b, rglru_lambda=rglru_lambda, out_norm_pool=out_norm_pool,
             out_norm_rnn=out_norm_rnn, w_out=w_out, norm_mem=norm_mem, w_mq=w_mq, w_mo=w_mo,
             norm_ffn=norm_ffn, ffn_w_gate=ffn_w_gate, ffn_w_up=ffn_w_up, ffn_w_down=ffn_w_down,
             router_w=router_w, moe_w_gate=moe_w_gate, moe_w_up=moe_w_up, moe_w_down=moe_w_down)
    bsz, t_len, d = x_prompt.shape
    n_dec = x_sample.shape[0]
    n_tok = bsz * t_len
    fg = final_norm.reshape(1, d)
    TM, TT, TM_DEC, BB = 512, 512, n_dec, SUBLANES

    xp = x_prompt
    xs = x_sample.reshape(n_dec, d)
    mem2 = mem_prompt.reshape(bsz * N_MEM, d)
    mem_ks, mem_vs, pools_p, convs_p, hs_p, pools_s, convs_s, hs_s = ([] for _ in range(8))
    zeros_hist = jnp.zeros((bsz, POOL_PAD, D_POOL), F32)
    zeros_conv = jnp.zeros((bsz, CONV_PAD, D_RNN), F32)
    zeros_h = jnp.zeros((bsz, 1, D_RNN), F32)

    for l in range(DEPTH):
        lw = _layer_weights(l, P)
        last_g = fg if l == DEPTH - 1 else None

        mg = mem_norm[l].reshape(1, d)
        mk = _norm_matmul(mem2, mg, w_mk[l].astype(BF16), TM).reshape(bsz, N_MEM, d)
        mv = _norm_matmul(mem2, mg, w_mv[l].astype(BF16), TM).reshape(bsz, N_MEM, d)
        mem_ks.append(mk.reshape(bsz, N_MEM, MEM_HEADS, MEM_HEAD_DIM))
        mem_vs.append(mv.reshape(bsz, N_MEM, MEM_HEADS, MEM_HEAD_DIM))

        proj = _norm_matmul(xp.reshape(n_tok, d), lw["norm_mix"], lw["w_in"], TM)
        proj = proj.reshape(bsz, t_len, D_IN)
        mixed, h_last = _mixer_prompt(proj, zeros_hist, zeros_conv, zeros_h, lw, 0, TT)
        pools_p.append(proj[:, t_len - POOL_HIST:, :D_POOL])
        convs_p.append(proj[:, t_len - (CONV_WIDTH - 1):, D_POOL:D_MIX])
        hs_p.append(h_last[:, 0, :])
        x2 = _matmul_res(mixed.reshape(n_tok, D_MIX), lw["w_out"], xp.reshape(n_tok, d), TM)
        x3 = _attn_prompt(x2.reshape(bsz, t_len, d), mk, mv, lw["norm_mem"], lw["w_mq"],
                          lw["w_mo"], TT)
        xp = _channel_mixer(x3.reshape(n_tok, d), l, lw, P, TM, last_g, True)
        xp = xp.reshape(bsz, t_len, d)

        proj_s = _norm_matmul(xs, lw["norm_mix"], lw["w_in"], TM_DEC)
        mixed_s, h_new = _mixer_decode(proj_s, jnp.swapaxes(state_pool[l], 0, 1),
                                       jnp.swapaxes(state_conv[l], 0, 1), state_h[l], lw,
                                       PAST_LEN)
        pools_s.append(jnp.concatenate([state_pool[l][:, 1:], proj_s[:, None, :D_POOL]], axis=1))
        convs_s.append(jnp.concatenate([state_conv[l][:, 1:], proj_s[:, None, D_POOL:D_MIX]],
                                       axis=1))
        hs_s.append(h_new)
        xs2 = _matmul_res(mixed_s, lw["w_out"], xs, TM_DEC)
        q_s = _norm_matmul(xs2, lw["norm_mem"], lw["w_mq"], TM_DEC)
        att = _attn_decode(q_s.reshape(n_dec, MEM_HEADS, MEM_HEAD_DIM), cache_mem_k, cache_mem_v,
                           l, BB)
        xs3 = _matmul_res(att.reshape(n_dec, d), lw["w_mo"], xs2, TM_DEC)
        xs = _channel_mixer(xs3, l, lw, P, TM_DEC, last_g, False)

    return (xp, xs.reshape(n_dec, 1, d), jnp.stack(pools_p), jnp.stack(convs_p), jnp.stack(hs_p),
            jnp.stack(mem_ks), jnp.stack(mem_vs), jnp.stack(pools_s), jnp.stack(convs_s),
            jnp.stack(hs_s))
```

```python
import functools
import math

import jax
import jax.numpy as jnp
from jax import lax
from jax.experimental import pallas as pl
from jax.experimental.pallas import tpu as pltpu

F32 = jnp.float32
BF16 = jnp.bfloat16

D_MODEL = 1024
DEPTH = 2
PAST_LEN = 16384
D_POOL = 512
POOL_WINDOWS = (2, 4, 8, 16)
POOL_GROUP = 128
POOL_HIST = 15
D_RNN = 512
RNN_HEADS = 8
CONV_WIDTH = 4
RGLRU_C = 8.0
D_MIX = 1024
D_IN = 1536
N_MEM = 256
MEM_HEADS = 4
MEM_HEAD_DIM = 256
N_EXPERTS = 8
D_FF_CHUNK = 1408
EPS = 1e-6

SUBLANES = 8
LANES = 128
POOL_PAD = 16
CONV_PAD = 8
VMEM_LIMIT = 56 * 1024 * 1024
SLOT_MAP_CHUNK = 8192
DISPATCH_CHUNK = 2048
TOK_BITS = 14
MOE_CHUNK = 256
MOE_TILE = 576


def _cparams(sem):
    return pltpu.CompilerParams(dimension_semantics=sem, vmem_limit_bytes=VMEM_LIMIT)


def _rms(x, g):
    ms = jnp.mean(x * x, axis=-1, keepdims=True)
    return x * lax.rsqrt(ms + EPS) * g


def _gelu_tanh(x):
    c = math.sqrt(2.0 / math.pi)
    return 0.5 * x * (1.0 + jnp.tanh(c * (x + 0.044715 * (x * x * x))))


def _softplus(x):
    return jnp.maximum(x, 0.0) + jnp.log1p(jnp.exp(-jnp.abs(x)))


def _norm_matmul_body(x_ref, g_ref, w_ref, o_ref):
    hn = _rms(x_ref[...], g_ref[...]).astype(BF16)
    o_ref[...] = jnp.dot(hn, w_ref[...], preferred_element_type=F32)


def _norm_matmul(x, g, w, tm):
    n, d = x.shape
    m = w.shape[1]
    return pl.pallas_call(
        _norm_matmul_body,
        grid=(n // tm,),
        in_specs=[pl.BlockSpec((tm, d), lambda i: (i, 0)),
                  pl.BlockSpec((1, d), lambda i: (0, 0)),
                  pl.BlockSpec((d, m), lambda i: (0, 0))],
        out_specs=pl.BlockSpec((tm, m), lambda i: (i, 0)),
        out_shape=jax.ShapeDtypeStruct((n, m), F32),
        compiler_params=_cparams(("parallel",)),
        name="norm_matmul",
    )(x, g, w)


def _mem_kv_body(x_ref, g_ref, wk_ref, wv_ref, k_ref, v_ref, kh_ref, vh_ref):
    hn = _rms(x_ref[...], g_ref[...]).astype(BF16)
    for w_ref, flat_ref, heads_ref in ((wk_ref, k_ref, kh_ref), (wv_ref, v_ref, vh_ref)):
        y = jnp.dot(hn, w_ref[...].astype(BF16), preferred_element_type=F32)
        flat_ref[...] = y
        heads_ref[...] = y.reshape(heads_ref.shape)


def _mem_kv(mem, g, wk, wv, tm):
    n, d = mem.shape
    n_layers = wk.shape[0]
    w_spec = pl.BlockSpec((None, d, d), lambda l, i: (l, 0, 0))
    flat = pl.BlockSpec((None, tm, d), lambda l, i: (l, i, 0))
    heads = pl.BlockSpec((None, tm, MEM_HEADS, MEM_HEAD_DIM), lambda l, i: (l, i, 0, 0))
    return pl.pallas_call(
        _mem_kv_body,
        grid=(n_layers, n // tm),
        in_specs=[pl.BlockSpec((tm, d), lambda l, i: (i, 0)),
                  pl.BlockSpec((None, 1, d), lambda l, i: (l, 0, 0)), w_spec, w_spec],
        out_specs=[flat, flat, heads, heads],
        out_shape=[jax.ShapeDtypeStruct((n_layers, n, d), F32)] * 2
        + [jax.ShapeDtypeStruct((n_layers, n, MEM_HEADS, MEM_HEAD_DIM), F32)] * 2,
        compiler_params=_cparams(("parallel", "parallel")),
        name="mem_kv",
    )(mem, g, wk, wv)


def _matmul_res_body(a_ref, w_ref, r_ref, o_ref):
    o_ref[...] = r_ref[...] + jnp.dot(a_ref[...].astype(BF16), w_ref[...],
                                      preferred_element_type=F32)


def _matmul_res(a, w, res, tm):
    n, k = a.shape
    m = w.shape[1]
    return pl.pallas_call(
        _matmul_res_body,
        grid=(n // tm,),
        in_specs=[pl.BlockSpec((tm, k), lambda i: (i, 0)),
                  pl.BlockSpec((k, m), lambda i: (0, 0)),
                  pl.BlockSpec((tm, m), lambda i: (i, 0))],
        out_specs=pl.BlockSpec((tm, m), lambda i: (i, 0)),
        out_shape=jax.ShapeDtypeStruct((n, m), F32),
        compiler_params=_cparams(("parallel",)),
        name="matmul_res",
    )(a, w, res)


def _pool_project(z_groups, pw_ref, scale):
    ys = [jnp.dot(z.astype(BF16), pw_ref[g], preferred_element_type=F32)
          for g, z in enumerate(z_groups)]
    return jnp.concatenate(ys, axis=-1) * scale


def _rglru_terms(c, first_pos, wa_ref, ba_ref, wx_ref, bx_ref, lam_ref):
    cb = c.astype(BF16)
    r = jax.nn.sigmoid(jnp.dot(cb, wa_ref[...], preferred_element_type=F32) + ba_ref[...])
    i = jax.nn.sigmoid(jnp.dot(cb, wx_ref[...], preferred_element_type=F32) + bx_ref[...])
    log_a = (-RGLRU_C) * r * _softplus(-lam_ref[...])
    a = jnp.exp(log_a)
    mult = jnp.sqrt(1.0 - a * a)
    if first_pos is not None:
        mult = jnp.where(first_pos, 1.0, mult)
    return a, mult * (i * c)


def _mix_out(y_pool, h, u_gate, gp_ref, gr_ref):
    y_rnn = h * _gelu_tanh(u_gate)
    return jnp.concatenate([_rms(y_pool, gp_ref[...]), _rms(y_rnn, gr_ref[...])], axis=-1)


def _mixer_body(pos0, tt, x_ref, gm_ref, win_ref, wout_ref, hist_ref, conv_ref, h0_ref,
                pw_ref, ps_ref, cw_ref, cb_ref, wa_ref, ba_ref, wx_ref, bx_ref, lam_ref, gp_ref,
                gr_ref, o_ref, hl_ref, ptail_ref, ctail_ref, proj_ref, pbuf, cbuf, a_s, b_s, hc):
    t = pl.program_id(1)
    hn = _rms(x_ref[...], gm_ref[...]).astype(BF16)
    proj_ref[...] = jnp.dot(hn, win_ref[...], preferred_element_type=F32)

    @pl.when(t == 0)
    def _():
        pbuf[0:POOL_PAD, :] = hist_ref[...]
        cbuf[0:CONV_PAD, :] = conv_ref[...]
        hc[...] = jnp.broadcast_to(h0_ref[...], (SUBLANES, D_RNN))

    @pl.when(t > 0)
    def _():
        pbuf[0:POOL_PAD, :] = pbuf[tt:tt + POOL_PAD, :]
        cbuf[0:CONV_PAD, :] = cbuf[tt:tt + CONV_PAD, :]

    u_pool = proj_ref[:, 0:D_POOL]
    pbuf[POOL_PAD:POOL_PAD + tt, :] = u_pool
    cbuf[CONV_PAD:CONV_PAD + tt, :] = proj_ref[:, D_POOL:D_MIX]

    pos = pos0 + t * tt + lax.broadcasted_iota(jnp.int32, (tt, 1), 0)

    zs = []
    for g, w in enumerate(POOL_WINDOWS):
        sl = slice(g * POOL_GROUP, (g + 1) * POOL_GROUP)
        acc = pbuf[POOL_PAD:POOL_PAD + tt, sl]
        for k in range(1, w):
            acc = acc + pbuf[POOL_PAD - k:POOL_PAD - k + tt, sl]
        cnt = jnp.minimum(pos + 1, w).astype(F32)
        zs.append(acc / cnt - u_pool[:, sl])
    y_pool = _pool_project(zs, pw_ref, ps_ref[...])

    c = cb_ref[...]
    for k in range(CONV_WIDTH):
        off = CONV_PAD - (CONV_WIDTH - 1) + k
        c = c + cbuf[off:off + tt, :] * cw_ref[k:k + 1, :]

    a, b = _rglru_terms(c, pos == 0, wa_ref, ba_ref, wx_ref, bx_ref, lam_ref)
    a_s[...] = a
    b_s[...] = b

    row = lax.broadcasted_iota(jnp.int32, (SUBLANES, D_RNN), 0)

    def group(j, carry):
        r0 = pl.multiple_of(j * SUBLANES, SUBLANES)
        ga = a_s[pl.ds(r0, SUBLANES), :]
        gb = b_s[pl.ds(r0, SUBLANES), :]
        for s in (1, 2, 4):
            keep = row >= s
            gb = jnp.where(keep, ga * pltpu.roll(gb, s, axis=0) + gb, gb)
            ga = jnp.where(keep, ga * pltpu.roll(ga, s, axis=0), ga)
        h = ga * carry + gb
        b_s[pl.ds(r0, SUBLANES), :] = h
        return jnp.broadcast_to(h[SUBLANES - 1:SUBLANES, :], (SUBLANES, D_RNN))

    carry = lax.fori_loop(0, tt // SUBLANES, group, hc[...], unroll=2)
    hc[...] = carry
    hl_ref[...] = carry

    mix = _mix_out(y_pool, b_s[...], proj_ref[:, D_MIX:D_IN], gp_ref, gr_ref)
    o_ref[...] = x_ref[...] + jnp.dot(mix.astype(BF16), wout_ref[...],
                                      preferred_element_type=F32)
    ptail_ref[...] = pbuf[tt:tt + POOL_PAD, :]
    ctail_ref[...] = cbuf[tt:tt + CONV_PAD, :]


def _mixer_prompt(x, hist, conv, h0, lw, pos0, tt):
    bsz, t_len, d = x.shape
    full = lambda shape: pl.BlockSpec(shape, lambda b, t: (0,) * len(shape))
    per_b = lambda rows, cols: pl.BlockSpec((None, rows, cols), lambda b, t: (b, 0, 0))
    xs = pl.BlockSpec((None, tt, d), lambda b, t: (b, t, 0))
    return pl.pallas_call(
        functools.partial(_mixer_body, pos0, tt),
        grid=(bsz, t_len // tt),
        in_specs=[xs, full((1, d)), full((d, D_IN)), full((D_MIX, d)),
                  per_b(POOL_PAD, D_POOL), per_b(CONV_PAD, D_RNN), per_b(1, D_RNN),
                  full((len(POOL_WINDOWS), POOL_GROUP, POOL_GROUP)), full((1, D_POOL)),
                  full((CONV_WIDTH, D_RNN)), full((1, D_RNN)),
                  full((D_RNN, D_RNN)), full((1, D_RNN)),
                  full((D_RNN, D_RNN)), full((1, D_RNN)),
                  full((1, D_RNN)), full((1, D_POOL)), full((1, D_RNN))],
        out_specs=[xs, per_b(SUBLANES, D_RNN), per_b(POOL_PAD, D_POOL), per_b(CONV_PAD, D_RNN)],
        out_shape=[jax.ShapeDtypeStruct((bsz, t_len, d), F32),
                   jax.ShapeDtypeStruct((bsz, SUBLANES, D_RNN), F32),
                   jax.ShapeDtypeStruct((bsz, POOL_PAD, D_POOL), F32),
                   jax.ShapeDtypeStruct((bsz, CONV_PAD, D_RNN), F32)],
        scratch_shapes=[pltpu.VMEM((tt, D_IN), F32),
                        pltpu.VMEM((POOL_PAD + tt, D_POOL), F32),
                        pltpu.VMEM((CONV_PAD + tt, D_RNN), F32),
                        pltpu.VMEM((tt, D_RNN), F32),
                        pltpu.VMEM((tt, D_RNN), F32),
                        pltpu.VMEM((SUBLANES, D_RNN), F32)],
        compiler_params=_cparams(("parallel", "arbitrary")),
        name="mixer_prompt",
    )(x, lw["norm_mix"], lw["w_in"], lw["w_out"], hist, conv, h0,
      lw["pool_w"], lw["pool_scale"], lw["conv_w"], lw["conv_b"],
      lw["gate_a_w"], lw["gate_a_b"], lw["gate_x_w"], lw["gate_x_b"], lw["lam"],
      lw["out_norm_pool"], lw["out_norm_rnn"])


def _mixer_decode_body(pos0, proj_ref, hist_ref, conv_ref, h0_ref, pw_ref, ps_ref, cw_ref, cb_ref,
                       wa_ref, ba_ref, wx_ref, bx_ref, lam_ref, gp_ref, gr_ref, mix_ref, hn_ref):
    u_pool = proj_ref[:, 0:D_POOL]
    zs = []
    for g, w in enumerate(POOL_WINDOWS):
        sl = slice(g * POOL_GROUP, (g + 1) * POOL_GROUP)
        acc = u_pool[:, sl]
        for k in range(1, w):
            acc = acc + hist_ref[POOL_HIST - k, :, sl]
        zs.append(acc / float(min(pos0 + 1, w)) - u_pool[:, sl])
    y_pool = _pool_project(zs, pw_ref, ps_ref[...])

    c = cb_ref[...]
    for k in range(CONV_WIDTH - 1):
        c = c + conv_ref[k] * cw_ref[k:k + 1, :]
    c = c + proj_ref[:, D_POOL:D_MIX] * cw_ref[CONV_WIDTH - 1:CONV_WIDTH, :]

    a, b = _rglru_terms(c, None, wa_ref, ba_ref, wx_ref, bx_ref, lam_ref)
    h = a * h0_ref[...] + b
    hn_ref[...] = h
    mix_ref[...] = _mix_out(y_pool, h, proj_ref[:, D_MIX:D_IN], gp_ref, gr_ref).astype(BF16)


def _mixer_decode(proj, hist_t, conv_t, h0, lw, pos0):
    assert pos0 > 0
    n = proj.shape[0]
    vm = pl.BlockSpec(memory_space=pltpu.VMEM)
    return pl.pallas_call(
        functools.partial(_mixer_decode_body, pos0),
        in_specs=[vm] * 15,
        out_specs=[vm, vm],
        out_shape=[jax.ShapeDtypeStruct((n, D_MIX), BF16),
                   jax.ShapeDtypeStruct((n, D_RNN), F32)],
        compiler_params=pltpu.CompilerParams(vmem_limit_bytes=VMEM_LIMIT),
        name="mixer_decode",
    )(proj, hist_t, conv_t, h0, lw["pool_w"], lw["pool_scale"], lw["conv_w"], lw["conv_b"],
      lw["gate_a_w"], lw["gate_a_b"], lw["gate_x_w"], lw["gate_x_b"], lw["lam"],
      lw["out_norm_pool"], lw["out_norm_rnn"])


def _softmax_rows(s):
    e = jnp.exp(s - jnp.max(s, axis=-1, keepdims=True))
    return e / jnp.sum(e, axis=-1, keepdims=True)


_NT = (((1,), (1,)), ((), ()))


def _attn_prompt_body(x_ref, k_ref, v_ref, g_ref, wq_ref, wo_ref, o_ref):
    x = x_ref[...]
    xn = _rms(x, g_ref[...]).astype(BF16)
    q = jnp.dot(xn, wq_ref[...], preferred_element_type=F32)
    outs = []
    for h in range(MEM_HEADS):
        sl = slice(h * MEM_HEAD_DIM, (h + 1) * MEM_HEAD_DIM)
        s = lax.dot_general(q[:, sl].astype(BF16), k_ref[:, sl].astype(BF16), _NT,
                            preferred_element_type=F32) * (MEM_HEAD_DIM ** -0.5)
        p = _softmax_rows(s).astype(BF16)
        outs.append(jnp.dot(p, v_ref[:, sl].astype(BF16), preferred_element_type=F32))
    o = jnp.concatenate(outs, axis=-1).astype(BF16)
    o_ref[...] = x + jnp.dot(o, wo_ref[...], preferred_element_type=F32)


def _attn_prompt(x, k, v, layer, g, wq, wo, tt):
    bsz, t_len, d = x.shape
    full = lambda shape: pl.BlockSpec(shape, lambda b, t: (0,) * len(shape))
    kv = pl.BlockSpec((None, None, N_MEM, d), lambda b, t: (layer, b, 0, 0))
    xs = pl.BlockSpec((None, tt, d), lambda b, t: (b, t, 0))
    return pl.pallas_call(
        _attn_prompt_body,
        grid=(bsz, t_len // tt),
        in_specs=[xs, kv, kv, full((1, d)), full((d, d)), full((d, d))],
        out_specs=xs,
        out_shape=jax.ShapeDtypeStruct(x.shape, F32),
        compiler_params=_cparams(("parallel", "parallel")),
        name="attn_prompt",
    )(x, k, v, g, wq, wo)


def _attn_decode_body(bb, q_ref, k_ref, v_ref, o_ref):
    for b in range(bb):
        k = k_ref[b]
        q = q_ref[b] * (MEM_HEAD_DIM ** -0.5)
        s = jnp.broadcast_to(jnp.sum(k * q[None], axis=-1, keepdims=True), k.shape)
        e = jnp.exp(s - jnp.max(s, axis=0, keepdims=True))
        o_ref[b] = jnp.sum(e * v_ref[b], axis=0) / jnp.sum(e, axis=0)


def _attn_decode(q, k, v, layer, bb):
    n, nh, hd = q.shape
    kv = pl.BlockSpec((None, bb, N_MEM, nh, hd), lambda i: (layer, i, 0, 0, 0))
    qs = pl.BlockSpec((bb, nh, hd), lambda i: (i, 0, 0))
    return pl.pallas_call(
        functools.partial(_attn_decode_body, bb),
        grid=(n // bb,),
        in_specs=[qs, kv, kv],
        out_specs=qs,
        out_shape=jax.ShapeDtypeStruct((n, nh, hd), F32),
        compiler_params=_cparams(("parallel",)),
        name="attn_decode",
    )(q, k, v)


def _top2(hn, rw_ref):
    rw = rw_ref[...]
    hn_hi, rw_hi = hn.astype(BF16), rw.astype(BF16)
    hn_lo = (hn - hn_hi.astype(F32)).astype(BF16)
    rw_lo = (rw - rw_hi.astype(F32)).astype(BF16)
    logits = (jnp.dot(hn_hi, rw_hi, preferred_element_type=F32)
              + jnp.dot(hn_lo, rw_hi, preferred_element_type=F32)
              + jnp.dot(hn_hi, rw_lo, preferred_element_type=F32))
    lane = lax.broadcasted_iota(jnp.int32, logits.shape, 1)
    neg = -jnp.inf
    lg = jnp.where(lane < N_EXPERTS, logits, neg)
    m1 = jnp.max(lg, axis=-1, keepdims=True)
    i1 = jnp.min(jnp.where(lg == m1, lane, LANES), axis=-1, keepdims=True)
    lg2 = jnp.where(lane == i1, neg, lg)
    m2 = jnp.max(lg2, axis=-1, keepdims=True)
    i2 = jnp.min(jnp.where(lg2 == m2, lane, LANES), axis=-1, keepdims=True)
    e2 = jnp.exp(m2 - m1)
    den = 1.0 + e2
    return lane, i1, i2, 1.0 / den, e2 / den


def _router_body(x_ref, g_ref, rw_ref, cw_ref):
    lane, i1, i2, g1, g2 = _top2(_rms(x_ref[...], g_ref[...]), rw_ref)
    cw_ref[...] = jnp.where(lane == i1, g1, 0.0) + jnp.where(lane == i2, g2, 0.0)


def _router(x, g, rw, tm):
    n, d = x.shape
    return pl.pallas_call(
        _router_body,
        grid=(n // tm,),
        in_specs=[pl.BlockSpec((tm, d), lambda i: (i, 0)),
                  pl.BlockSpec((1, d), lambda i: (0, 0)),
                  pl.BlockSpec((d, LANES), lambda i: (0, 0))],
        out_specs=pl.BlockSpec((tm, LANES), lambda i: (i, 0)),
        out_shape=jax.ShapeDtypeStruct((n, LANES), F32),
        compiler_params=_cparams(("parallel",)),
        name="router",
    )(x, g, rw)


def _store_row_tiles(ref, val):
    rows = val.shape[0]
    for c in range(val.shape[1] // LANES):
        ref[pl.ds(c, rows, stride=SUBLANES), :] = val[:, c * LANES:(c + 1) * LANES]


def _load_row_tiles(ref, rows):
    return jnp.concatenate([ref[pl.ds(c, rows, stride=SUBLANES), :] for c in range(SUBLANES)],
                           axis=-1)


R_E1, R_E2, R_RANK1, R_RANK2, R_G1, R_G2 = range(6)


def _router_rank_body(x_ref, g_ref, rw_ref, rec_ref, rows_ref, cnt_ref, hn_ref, carry):
    i = pl.program_id(0)
    tm = x_ref.shape[0]

    @pl.when(i == 0)
    def _():
        carry[...] = jnp.zeros_like(carry)

    hn = _rms(x_ref[...], g_ref[...])
    _store_row_tiles(hn_ref, hn)
    lane, i1, i2, g1, g2 = _top2(hn, rw_ref)
    picks = ((lane == i1) | (lane == i2)).astype(F32)
    r_io = lax.broadcasted_iota(jnp.int32, (tm, tm), 0)
    c_io = lax.broadcasted_iota(jnp.int32, (tm, tm), 1)
    earlier = (c_io < r_io).astype(BF16)
    before = jnp.dot(earlier, picks.astype(BF16), preferred_element_type=F32) + carry[...]
    rank1 = jnp.sum(jnp.where(lane == i1, before, 0.0), axis=-1, keepdims=True)
    rank2 = jnp.sum(jnp.where(lane == i2, before, 0.0), axis=-1, keepdims=True)
    carry[...] += jnp.sum(picks, axis=0, keepdims=True)
    cnt_ref[...] = jnp.broadcast_to(carry[...], cnt_ref.shape)

    rec = jnp.zeros((tm, LANES), F32)
    for k, v in ((R_E1, i1.astype(F32)), (R_E2, i2.astype(F32)), (R_RANK1, rank1),
                 (R_RANK2, rank2), (R_G1, g1), (R_G2, g2)):
        rec = jnp.where(lane == k, v, rec)
    rec_ref[...] = rec
    rows_ref[...] = rec.T[0:SUBLANES, :].astype(jnp.int32)


def _router_rank(x, g, rw, tm):
    n, d = x.shape
    assert d == SUBLANES * LANES
    return pl.pallas_call(
        _router_rank_body,
        grid=(n // tm,),
        in_specs=[pl.BlockSpec((tm, d), lambda i: (i, 0)),
                  pl.BlockSpec((1, d), lambda i: (0, 0)),
                  pl.BlockSpec((d, LANES), lambda i: (0, 0))],
        out_specs=[pl.BlockSpec((tm, LANES), lambda i: (i, 0)),
                   pl.BlockSpec((SUBLANES, tm), lambda i: (0, i)),
                   pl.BlockSpec((SUBLANES, LANES), lambda i: (0, 0)),
                   pl.BlockSpec((tm * SUBLANES, LANES), lambda i: (i, 0))],
        out_shape=[jax.ShapeDtypeStruct((n, LANES), F32),
                   jax.ShapeDtypeStruct((SUBLANES, n), jnp.int32),
                   jax.ShapeDtypeStruct((SUBLANES, LANES), F32),
                   jax.ShapeDtypeStruct((n * SUBLANES, LANES), F32)],
        scratch_shapes=[pltpu.VMEM((1, LANES), F32)],
        compiler_params=_cparams(("arbitrary",)),
        name="router_rank",
    )(x, g, rw)


def _slot_map_body(n_tok, tm, n_tiles, cnt_ref, rows_ref, slot_ref, te_ref, nu_ref, off_s,
                   pos_v, pos1_s, pos2_s):
    c = pl.program_id(0)
    tc = rows_ref.shape[1]

    def fill_pad(s, ordinal):
        slot_ref[s] = (2 * n_tok + ordinal) << TOK_BITS
        return ordinal + 1

    @pl.when(c == 0)
    def _():
        lead = lax.fori_loop(0, tm, fill_pad, jnp.int32(0))

        def per_expert(e, carry):
            tile0, ordinal = carry
            n = cnt_ref[e]
            start = (tile0 + 1) * tm
            off_s[e] = start
            tile1 = tile0 + lax.div(n + (tm - 1), tm)

            def fill_tile(t, z):
                te_ref[t] = e
                return z

            lax.fori_loop(tile0, tile1, fill_tile, 0)
            ordinal = lax.fori_loop(start + n, (tile1 + 1) * tm, fill_pad, ordinal)
            return tile1, ordinal

        used, ordinal = lax.fori_loop(0, N_EXPERTS, per_expert, (jnp.int32(0), lead))
        nu_ref[0] = used

        def fill_rest(t, z):
            te_ref[t] = N_EXPERTS - 1
            return z

        lax.fori_loop(used, n_tiles, fill_rest, 0)
        lax.fori_loop((used + 1) * tm, (n_tiles + 2) * tm, fill_pad, ordinal)

    rows = rows_ref[...]
    pos = rows[R_RANK1:R_RANK2 + 1, :]
    for e in range(N_EXPERTS):
        pos = pos + jnp.where(rows[R_E1:R_E2 + 1, :] == e, off_s[e], 0)
    pos_v[...] = pos
    pltpu.sync_copy(pos_v.at[0], pos1_s)
    pltpu.sync_copy(pos_v.at[1], pos2_s)

    step = (1 << TOK_BITS) + 1

    def per_token(j, first):
        slot_ref[pos1_s[j]] = first
        slot_ref[pos2_s[j]] = first + (n_tok << TOK_BITS)
        return first + step

    lax.fori_loop(0, tc, per_token, (c * tc) * step, unroll=8)


def _slot_map(counts, rows, n_tok, tm, n_tiles, tc):
    smem = pl.BlockSpec(memory_space=pltpu.SMEM)
    return pl.pallas_call(
        functools.partial(_slot_map_body, n_tok, tm, n_tiles),
        grid=(n_tok // tc,),
        in_specs=[smem, pl.BlockSpec((SUBLANES, tc), lambda c: (0, c))],
        out_specs=[smem, smem, smem],
        out_shape=[jax.ShapeDtypeStruct(((n_tiles + 2) * tm,), jnp.int32),
                   jax.ShapeDtypeStruct((n_tiles,), jnp.int32),
                   jax.ShapeDtypeStruct((1,), jnp.int32)],
        scratch_shapes=[pltpu.SMEM((N_EXPERTS,), jnp.int32), pltpu.VMEM((2, tc), jnp.int32),
                        pltpu.SMEM((tc,), jnp.int32), pltpu.SMEM((tc,), jnp.int32)],
        compiler_params=_cparams(("arbitrary",)),
        name="slot_map",
    )(counts, rows)


def _experts_body(tm, slot_ref, te_ref, nu_ref, x_hbm, wg_ref, wu_ref, wd_ref,
                  out_hbm, xbuf, ybuf, hn_s, mid_s, acc, gsem, ssem, zsem):
    del te_ref
    i = pl.program_id(0)
    n_used = nu_ref[0]
    cur = i % 2
    nxt = 1 - cur

    def tile_rows(r):
        return pl.ds(pl.multiple_of(r * SUBLANES, SUBLANES), SUBLANES)

    def gather_copy(src_row, r, buf):
        return pltpu.make_async_copy(x_hbm.at[tile_rows(src_row), :],
                                     xbuf.at[buf, tile_rows(r), :], gsem.at[buf])

    def scatter_copy(dst_row, r, buf):
        return pltpu.make_async_copy(ybuf.at[buf, tile_rows(r), :],
                                     out_hbm.at[tile_rows(dst_row), :], ssem.at[buf])

    def gather_src(tile, r):
        return slot_ref[(tile + 1) * tm + r] & ((1 << TOK_BITS) - 1)

    def scatter_dst(tile, r):
        return slot_ref[(tile + 1) * tm + r] >> TOK_BITS

    def wait_gathers(buf):
        for r in range(tm):
            gather_copy(0, r, buf).wait()

    def wait_scatters(buf):
        for r in range(tm):
            scatter_copy(0, r, buf).wait()

    @pl.when(i == 0)
    def _():
        ybuf[...] = jnp.zeros_like(ybuf)

        def zero_tile(j, z):
            rows = pl.ds(pl.multiple_of(j * (tm * SUBLANES), SUBLANES), tm * SUBLANES)
            cp = pltpu.make_async_copy(ybuf.at[0], out_hbm.at[rows, :], zsem.at[0])
            cp.start()
            cp.wait()
            return z

        lax.fori_loop(n_used + 1, out_hbm.shape[0] // (tm * SUBLANES), zero_tile, 0)
        for r in range(tm):
            gather_copy(gather_src(0, r), r, 0).start()

    @pl.when(i < n_used)
    def _():
        wait_gathers(cur)
        hn_s[...] = _load_row_tiles(xbuf.at[cur], tm).astype(BF16)
        acc[...] = jnp.zeros_like(acc)
        mid_s[...] = jnp.zeros_like(mid_s)
        n_chunks = wg_ref.shape[0]
        rows_per = tm // n_chunks

        def chunk(k, carry):
            row0 = pl.multiple_of(k * rows_per, SUBLANES)
            gbase = (i + 2) * tm + row0
            sbase = i * tm + row0
            for r in range(rows_per):
                src = slot_ref[gbase + r] & ((1 << TOK_BITS) - 1)
                gather_copy(src, row0 + r, nxt).start()
                scatter_copy(slot_ref[sbase + r] >> TOK_BITS, row0 + r, nxt).start()
            acc[...] += jnp.dot(mid_s[...], wd_ref[jnp.maximum(k - 1, 0)],
                                preferred_element_type=F32)
            hn = hn_s[...]
            gate = jnp.dot(hn, wg_ref[k], preferred_element_type=F32)
            up = jnp.dot(hn, wu_ref[k], preferred_element_type=F32)
            mid_s[...] = (gate * jax.nn.sigmoid(gate) * up).astype(BF16)
            return carry

        lax.fori_loop(0, n_chunks, chunk, 0)
        y = acc[...] + jnp.dot(mid_s[...], wd_ref[n_chunks - 1], preferred_element_type=F32)

        @pl.when(i >= 1)
        def _():
            wait_scatters(cur)

        _store_row_tiles(ybuf.at[cur], y)

    @pl.when(i == n_used)
    def _():
        wait_gathers(cur)
        wait_scatters(cur)
        for r in range(tm):
            scatter_copy(scatter_dst(i - 1, r), r, nxt).start()
        wait_scatters(nxt)


def _experts_sparse(xt, wg, wu, wd, slot, tile_expert, n_used, tm, n_tiles):
    n_tok = xt.shape[0] // SUBLANES
    _, n_chunks, d, fc = wg.shape
    n_pad = (n_tiles + 2) * tm - 2 * n_tok
    assert n_tok <= 1 << TOK_BITS and 2 * n_tok + n_pad <= 1 << (31 - TOK_BITS)
    assert tm % (n_chunks * SUBLANES) == 0
    w_idx = lambda i, slot, te, nu: (te[jnp.minimum(i, n_tiles - 1)], 0, 0, 0)
    grid_spec = pltpu.PrefetchScalarGridSpec(
        num_scalar_prefetch=3,
        grid=(n_tiles + 1,),
        in_specs=[pl.BlockSpec(memory_space=pl.ANY),
                  pl.BlockSpec((None, n_chunks, d, fc), w_idx),
                  pl.BlockSpec((None, n_chunks, d, fc), w_idx),
                  pl.BlockSpec((None, n_chunks, fc, d), w_idx)],
        out_specs=pl.BlockSpec(memory_space=pl.ANY),
        scratch_shapes=[pltpu.VMEM((2, tm * SUBLANES, LANES), F32),
                        pltpu.VMEM((2, tm * SUBLANES, LANES), F32),
                        pltpu.VMEM((tm, d), BF16), pltpu.VMEM((tm, fc), BF16),
                        pltpu.VMEM((tm, d), F32),
                        pltpu.SemaphoreType.DMA((2,)), pltpu.SemaphoreType.DMA((2,)),
                        pltpu.SemaphoreType.DMA((1,))])
    return pl.pallas_call(
        functools.partial(_experts_body, tm),
        grid_spec=grid_spec,
        out_shape=jax.ShapeDtypeStruct(((2 * n_tok + n_pad) * SUBLANES, LANES), F32),
        compiler_params=_cparams(("arbitrary",)),
        name="experts_sparse",
    )(slot, tile_expert, n_used, xt, wg, wu, wd)


def _combine_body(final, x_ref, y1_ref, y2_ref, rec_ref, *rest):
    o_ref = rest[-1]
    tm = x_ref.shape[0]
    rec = rec_ref[...]
    out = (x_ref[...] + rec[:, R_G1:R_G1 + 1] * _load_row_tiles(y1_ref, tm)
           + rec[:, R_G2:R_G2 + 1] * _load_row_tiles(y2_ref, tm))
    o_ref[...] = _rms(out, rest[0][...]) if final else out


def _combine(x, y, rec, tm, final_g=None):
    n, d = x.shape
    final = final_g is not None
    nb = n // tm
    row = pl.BlockSpec((tm, d), lambda i: (i, 0))
    in_specs = [row, pl.BlockSpec((tm * SUBLANES, LANES), lambda i: (i, 0)),
                pl.BlockSpec((tm * SUBLANES, LANES), lambda i: (nb + i, 0)),
                pl.BlockSpec((tm, LANES), lambda i: (i, 0))]
    args = [x, y, y, rec]
    if final:
        in_specs.append(pl.BlockSpec((1, d), lambda i: (0, 0)))
        args.append(final_g)
    return pl.pallas_call(
        functools.partial(_combine_body, final),
        grid=(n // tm,),
        in_specs=in_specs,
        out_specs=row,
        out_shape=jax.ShapeDtypeStruct((n, d), F32),
        compiler_params=_cparams(("parallel",)),
        name="moe_combine",
    )(*args)


def _tile_rows(r):
    return pl.ds(pl.multiple_of(r * SUBLANES, SUBLANES), SUBLANES)


def _route_plan_body(tm, n_tiles, cnt_ref, rows_ref, pos_ref, te_ref, nu_ref, last_ref, off_s):
    c = pl.program_id(0)

    @pl.when(c == 0)
    def _():
        def per_expert(e, tile0):
            n = cnt_ref[e]
            off_s[e] = tile0 * tm
            tile1 = tile0 + lax.div(n + (tm - 1), tm)

            def fill_tile(t, z):
                te_ref[t] = e
                return z

            lax.fori_loop(tile0, tile1, fill_tile, 0)
            last_ref[e] = jnp.where(tile1 > tile0, tile1 - 1, -1)
            return tile1

        used = lax.fori_loop(0, N_EXPERTS, per_expert, jnp.int32(0))
        nu_ref[0] = used

        def fill_rest(t, z):
            te_ref[t] = N_EXPERTS - 1
            return z

        lax.fori_loop(used, n_tiles, fill_rest, 0)

    rows = rows_ref[...]
    pos = rows[R_RANK1:R_RANK2 + 1, :]
    for e in range(N_EXPERTS):
        pos = pos + jnp.where(rows[R_E1:R_E2 + 1, :] == e, off_s[e], 0)
    pos_ref[...] = jnp.zeros_like(pos_ref)
    pos_ref[0:2, :] = pos


def _route_plan(counts, rows, tm, n_tiles, tc):
    n_tok = rows.shape[1]
    smem = pl.BlockSpec(memory_space=pltpu.SMEM)
    blk = pl.BlockSpec((SUBLANES, tc), lambda c: (0, c))
    return pl.pallas_call(
        functools.partial(_route_plan_body, tm, n_tiles),
        grid=(n_tok // tc,),
        in_specs=[smem, blk],
        out_specs=[blk, smem, smem, smem],
        out_shape=[jax.ShapeDtypeStruct((SUBLANES, n_tok), jnp.int32),
                   jax.ShapeDtypeStruct((n_tiles,), jnp.int32),
                   jax.ShapeDtypeStruct((1,), jnp.int32),
                   jax.ShapeDtypeStruct((N_EXPERTS,), jnp.int32)],
        scratch_shapes=[pltpu.SMEM((N_EXPERTS,), jnp.int32)],
        compiler_params=_cparams(("arbitrary",)),
        name="route_plan",
    )(counts, rows)


def _dispatch_body(tm, n_tiles, last_ref, nu_ref, pos_ref, xt_hbm, xs_hbm, zbuf, sem, zsem):
    c = pl.program_id(0)
    tc = pos_ref.shape[1]

    def zero_tile(t):
        rows = pl.ds(pl.multiple_of(t * (tm * SUBLANES), SUBLANES), tm * SUBLANES)
        cp = pltpu.make_async_copy(zbuf, xs_hbm.at[rows, :], zsem.at[0])
        cp.start()
        cp.wait()

    @pl.when(c == 0)
    def _():
        zbuf[...] = jnp.zeros_like(zbuf)
        for e in range(N_EXPERTS):
            @pl.when(last_ref[e] >= 0)
            def _():
                zero_tile(last_ref[e])

        def zero_unused(t, z):
            zero_tile(t)
            return z

        lax.fori_loop(nu_ref[0], n_tiles, zero_unused, 0)

    def row_copy(t, p):
        return pltpu.make_async_copy(xt_hbm.at[_tile_rows(t), :], xs_hbm.at[_tile_rows(p), :],
                                     sem.at[0])

    def issue(j, z):
        t = c * tc + j
        row_copy(t, pos_ref[0, j]).start()
        row_copy(t, pos_ref[1, j]).start()
        return z

    lax.fori_loop(0, tc, issue, 0, unroll=8)

    def drain(j, z):
        row_copy(0, 0).wait()
        row_copy(0, 0).wait()
        return z

    lax.fori_loop(0, tc, drain, 0, unroll=8)


def _dispatch(xt, pos, last, n_used, tm, n_tiles, tc):
    n_tok = xt.shape[0] // SUBLANES
    smem = pl.BlockSpec(memory_space=pltpu.SMEM)
    return pl.pallas_call(
        functools.partial(_dispatch_body, tm, n_tiles),
        grid=(n_tok // tc,),
        in_specs=[smem, smem,
                  pl.BlockSpec((SUBLANES, tc), lambda c: (0, c), memory_space=pltpu.SMEM),
                  pl.BlockSpec(memory_space=pl.ANY)],
        out_specs=pl.BlockSpec(memory_space=pl.ANY),
        out_shape=jax.ShapeDtypeStruct((n_tiles * tm * SUBLANES, LANES), F32),
        scratch_shapes=[pltpu.VMEM((tm * SUBLANES, LANES), F32),
                        pltpu.SemaphoreType.DMA((1,)), pltpu.SemaphoreType.DMA((1,))],
        compiler_params=_cparams(("arbitrary",)),
        name="moe_dispatch",
    )(last, n_used, pos, xt)


def _grouped_experts_body(tm, te_ref, nu_ref, x_ref, wg_ref, wu_ref, wd_ref, y_ref):
    del te_ref
    i = pl.program_id(0)

    @pl.when(i < nu_ref[0])
    def _():
        hn = _load_row_tiles(x_ref, tm).astype(BF16)
        acc = None
        for k in range(wg_ref.shape[0]):
            gate = jnp.dot(hn, wg_ref[k], preferred_element_type=F32)
            up = jnp.dot(hn, wu_ref[k], preferred_element_type=F32)
            mid = (gate * jax.nn.sigmoid(gate) * up).astype(BF16)
            y = jnp.dot(mid, wd_ref[k], preferred_element_type=F32)
            acc = y if acc is None else acc + y
        _store_row_tiles(y_ref, acc)

    @pl.when(i >= nu_ref[0])
    def _():
        y_ref[...] = jnp.zeros_like(y_ref)


def _grouped_experts(xs, wg, wu, wd, tile_expert, n_used, tm, n_tiles):
    _, n_chunks, d, fc = wg.shape
    w_idx = lambda i, te, nu: (te[i], 0, 0, 0)
    grid_spec = pltpu.PrefetchScalarGridSpec(
        num_scalar_prefetch=2,
        grid=(n_tiles,),
        in_specs=[pl.BlockSpec((tm * SUBLANES, LANES),
                               lambda i, te, nu: (jnp.minimum(i, nu[0] - 1), 0)),
                  pl.BlockSpec((None, n_chunks, d, fc), w_idx),
                  pl.BlockSpec((None, n_chunks, d, fc), w_idx),
                  pl.BlockSpec((None, n_chunks, fc, d), w_idx)],
        out_specs=pl.BlockSpec((tm * SUBLANES, LANES), lambda i, te, nu: (i, 0)))
    return pl.pallas_call(
        functools.partial(_grouped_experts_body, tm),
        grid_spec=grid_spec,
        out_shape=jax.ShapeDtypeStruct(xs.shape, F32),
        compiler_params=_cparams(("arbitrary",)),
        name="grouped_experts",
    )(tile_expert, n_used, xs, wg, wu, wd)


def _gather_combine_body(final, pos_ref, posn_ref, x_ref, rec_ref, y_hbm, *rest):
    o_ref, y1buf, y2buf, sem = rest[-4:]
    i = pl.program_id(0)
    tm = x_ref.shape[0]
    cur = i % 2

    def row_copy(p, r, dst, buf):
        return pltpu.make_async_copy(y_hbm.at[_tile_rows(p), :], dst.at[buf, _tile_rows(r), :],
                                     sem.at[buf])

    def issue(pref, buf):
        def one(j, z):
            row_copy(pref[0, j], j, y1buf, buf).start()
            row_copy(pref[1, j], j, y2buf, buf).start()
            return z

        lax.fori_loop(0, tm, one, 0, unroll=8)

    @pl.when(i == 0)
    def _():
        issue(pos_ref, 0)

    @pl.when(i + 1 < pl.num_programs(0))
    def _():
        issue(posn_ref, 1 - cur)

    def drain(j, z):
        row_copy(0, 0, y1buf, cur).wait()
        row_copy(0, 0, y2buf, cur).wait()
        return z

    lax.fori_loop(0, tm, drain, 0, unroll=8)
    rec = rec_ref[...]
    out = (x_ref[...] + rec[:, R_G1:R_G1 + 1] * _load_row_tiles(y1buf.at[cur], tm)
           + rec[:, R_G2:R_G2 + 1] * _load_row_tiles(y2buf.at[cur], tm))
    o_ref[...] = _rms(out, rest[0][...]) if final else out


def _gather_combine(x, y, rec, pos, tm, final_g=None):
    n, d = x.shape
    final = final_g is not None
    nb = n // tm
    row = pl.BlockSpec((tm, d), lambda i: (i, 0))
    in_specs = [pl.BlockSpec((SUBLANES, tm), lambda i: (0, i), memory_space=pltpu.SMEM),
                pl.BlockSpec((SUBLANES, tm), lambda i: (0, jnp.minimum(i + 1, nb - 1)),
                             memory_space=pltpu.SMEM),
                row, pl.BlockSpec((tm, LANES), lambda i: (i, 0)),
                pl.BlockSpec(memory_space=pl.ANY)]
    args = [pos, pos, x, rec, y]
    if final:
        in_specs.append(pl.BlockSpec((1, d), lambda i: (0, 0)))
        args.append(final_g)
    return pl.pallas_call(
        functools.partial(_gather_combine_body, final),
        grid=(nb,),
        in_specs=in_specs,
        out_specs=row,
        out_shape=jax.ShapeDtypeStruct((n, d), F32),
        scratch_shapes=[pltpu.VMEM((2, tm * SUBLANES, LANES), F32),
                        pltpu.VMEM((2, tm * SUBLANES, LANES), F32),
                        pltpu.SemaphoreType.DMA((2,))],
        compiler_params=_cparams(("arbitrary",)),
        name="moe_gather_combine",
    )(*args)


def _ffn_body(n_chunks, per_expert, final, *refs):
    routed = per_expert is not None
    x_ref, g_ref, wg_ref, wu_ref, wd_ref = refs[:5]
    rest = list(refs[5:])
    cw_ref = rest.pop(0) if routed else None
    fg_ref = rest.pop(0) if final else None
    o_ref, hn_s = rest
    j = pl.program_id(1)

    @pl.when(j == 0)
    def _():
        hn_s[...] = _rms(x_ref[...], g_ref[...]).astype(BF16)
        o_ref[...] = x_ref[...]

    hn = hn_s[...]

    def swiglu(wg, wu, wd):
        gate = jnp.dot(hn, wg, preferred_element_type=F32)
        up = jnp.dot(hn, wu, preferred_element_type=F32)
        mid = (gate * jax.nn.sigmoid(gate) * up).astype(BF16)
        return jnp.dot(mid, wd, preferred_element_type=F32)

    if routed:
        y = swiglu(wg_ref[0], wu_ref[0], wd_ref[0])
        for c in range(1, per_expert):
            y = y + swiglu(wg_ref[c], wu_ref[c], wd_ref[c])
        cw = cw_ref[...]
        lane = lax.broadcasted_iota(jnp.int32, cw.shape, 1)
        y = y * jnp.sum(jnp.where(lane == j, cw, 0.0), axis=-1, keepdims=True)
    else:
        y = swiglu(wg_ref[...], wu_ref[...], wd_ref[...])
    o_ref[...] += y

    if final:
        @pl.when(j == n_chunks - 1)
        def _():
            o_ref[...] = _rms(o_ref[...], fg_ref[...])


def _ffn(x, g, wg, wu, wd, tm, cw=None, final_g=None):
    n, d = x.shape
    routed = cw is not None
    final = final_g is not None
    row = pl.BlockSpec((tm, d), lambda i, j: (i, 0))
    vec = pl.BlockSpec((1, d), lambda i, j: (0, 0))
    if routed:
        n_chunks, per_expert, _, fc = wg.shape
        w_idx = lambda i, j: (j, 0, 0, 0)
        w_in_spec = pl.BlockSpec((None, per_expert, d, fc), w_idx)
        in_specs = [row, vec, w_in_spec, w_in_spec,
                    pl.BlockSpec((None, per_expert, fc, d), w_idx),
                    pl.BlockSpec((tm, LANES), lambda i, j: (i, 0))]
        args = [x, g, wg, wu, wd, cw]
    else:
        fc = D_FF_CHUNK
        per_expert = None
        n_chunks = wd.shape[0] // fc
        w_in_spec = pl.BlockSpec((d, fc), lambda i, j: (0, j))
        in_specs = [row, vec, w_in_spec, w_in_spec, pl.BlockSpec((fc, d), lambda i, j: (j, 0))]
        args = [x, g, wg, wu, wd]
    if final:
        in_specs.append(vec)
        args.append(final_g)
    return pl.pallas_call(
        functools.partial(_ffn_body, n_chunks, per_expert, final),
        grid=(n // tm, n_chunks),
        in_specs=in_specs,
        out_specs=row,
        out_shape=jax.ShapeDtypeStruct((n, d), F32),
        scratch_shapes=[pltpu.VMEM((tm, d), BF16)],
        compiler_params=_cparams(("parallel", "arbitrary")),
        name="ffn",
    )(*args)


def _block_diag(w):
    h, a, b = w.shape
    eye = jnp.eye(h, dtype=w.dtype)
    return (eye[:, None, :, None] * w[:, :, None, :]).reshape(h * a, h * b)


def _layer_weights(l, P):
    row = lambda v: v[l].reshape(1, -1)
    return dict(
        norm_mix=row(P["norm_mix"]), w_in=P["w_in"][l].astype(BF16),
        pool_w=P["pool_w"][l].astype(BF16), pool_scale=row(P["pool_scale"]),
        conv_w=P["conv_w"][l], conv_b=row(P["conv_b"]),
        gate_a_w=_block_diag(P["gate_a_w"][l]).astype(BF16), gate_a_b=row(P["gate_a_b"]),
        gate_x_w=_block_diag(P["gate_x_w"][l]).astype(BF16), gate_x_b=row(P["gate_x_b"]),
        lam=row(P["rglru_lambda"]),
        out_norm_pool=row(P["out_norm_pool"]), out_norm_rnn=row(P["out_norm_rnn"]),
        w_out=P["w_out"][l].astype(BF16), norm_mem=row(P["norm_mem"]),
        w_mq=P["w_mq"][l].astype(BF16), w_mo=P["w_mo"][l].astype(BF16),
        norm_ffn=row(P["norm_ffn"]),
        moe=(_chunk_experts(P["moe_w_gate"][l // 2], P["moe_w_up"][l // 2], P["moe_w_down"][l // 2])
             if l % 2 == 1 else None))


def _channel_mixer(x, l, lw, P, tm, final_g, sparse):
    j = l // 2
    n = x.shape[0]
    if l % 2 == 0:
        return _ffn(x, lw["norm_ffn"], P["ffn_w_gate"][j].astype(BF16),
                    P["ffn_w_up"][j].astype(BF16), P["ffn_w_down"][j].astype(BF16),
                    tm, final_g=final_g)
    rw = jnp.pad(P["router_w"][j], ((0, 0), (0, LANES - N_EXPERTS)))
    wg, wu, wd = lw["moe"]
    if not sparse:
        cw = _router(x, lw["norm_ffn"], rw, tm)
        return _ffn(x, lw["norm_ffn"], wg, wu, wd, tm, cw=cw, final_g=final_g)
    te = MOE_TILE
    n_tiles = -(-2 * n // te) + N_EXPERTS
    rec, rows, cnt, xt = _router_rank(x, lw["norm_ffn"], rw, tm)
    counts = cnt[0, :N_EXPERTS].astype(jnp.int32)
    pos, tile_expert, n_used, last = _route_plan(counts, rows, te, n_tiles, DISPATCH_CHUNK)
    xs = _dispatch(xt, pos, last, n_used, te, n_tiles, DISPATCH_CHUNK)
    y = _grouped_experts(xs, wg, wu, wd, tile_expert, n_used, te, n_tiles)
    return _gather_combine(x, y, rec, pos, tm, final_g=final_g)


def _chunk_experts(wg, wu, wd):
    e, d, f = wg.shape
    c = -(-f // MOE_CHUNK)
    fc = MOE_CHUNK

    def body(wg_ref, wu_ref, wd_ref, og_ref, ou_ref, od_ref):
        valid = f - pl.program_id(1) * fc
        col_ok = lax.broadcasted_iota(jnp.int32, (d, fc), 1) < valid
        row_ok = lax.broadcasted_iota(jnp.int32, (fc, d), 0) < valid
        og_ref[...] = jnp.where(col_ok, wg_ref[...], 0.0).astype(BF16)
        ou_ref[...] = jnp.where(col_ok, wu_ref[...], 0.0).astype(BF16)
        od_ref[...] = jnp.where(row_ok, wd_ref[...], 0.0).astype(BF16)

    cols_in = pl.BlockSpec((None, d, fc), lambda i, j: (i, 0, j))
    rows_in = pl.BlockSpec((None, fc, d), lambda i, j: (i, j, 0))
    cols_out = pl.BlockSpec((None, None, d, fc), lambda i, j: (i, j, 0, 0))
    rows_out = pl.BlockSpec((None, None, fc, d), lambda i, j: (i, j, 0, 0))
    return pl.pallas_call(
        body,
        grid=(e, c),
        in_specs=[cols_in, cols_in, rows_in],
        out_specs=[cols_out, cols_out, rows_out],
        out_shape=[jax.ShapeDtypeStruct((e, c, d, fc), BF16),
                   jax.ShapeDtypeStruct((e, c, d, fc), BF16),
                   jax.ShapeDtypeStruct((e, c, fc, d), BF16)],
        compiler_params=_cparams(("parallel", "parallel")),
        name="chunk_experts",
    )(wg, wu, wd)


def kernel(x_prompt, x_sample, mem_prompt, state_pool, state_conv, state_h, cache_mem_k, cache_mem_v,
           norm_mix, w_in, pool_w, pool_scale, conv_w, conv_b, gate_a_w, gate_a_b, gate_x_w, gate_x_b,
           rglru_lambda, out_norm_pool, out_norm_rnn, w_out, norm_mem, mem_norm, w_mq, w_mk, w_mv, w_mo,
           norm_ffn, ffn_w_gate, ffn_w_up, ffn_w_down, router_w, moe_w_gate, moe_w_up, moe_w_down,
           final_norm):
    P = dict(norm_mix=norm_mix, w_in=w_in, pool_w=pool_w, pool_scale=pool_scale, conv_w=conv_w,
             conv_b=conv_b, gate_a_w=gate_a_w, gate_a_b=gate_a_b, gate_x_w=gate_x_w,
             gate_x_b=gate_x_b, rglru_lambda=rglru_lambda, out_norm_pool=out_norm_pool,
             out_norm_rnn=out_norm_rnn, w_out=w_out, norm_mem=norm_mem, w_mq=w_mq, w_mo=w_mo,
             norm_ffn=norm_ffn, ffn_w_gate=ffn_w_gate, ffn_w_up=ffn_w_up, ffn_w_down=ffn_w_down,
             router_w=router_w, moe_w_gate=moe_w_gate, moe_w_up=moe_w_up, moe_w_down=moe_w_down)
    bsz, t_len, d = x_prompt.shape
    n_dec = x_sample.shape[0]
    n_tok = bsz * t_len
    fg = final_norm.reshape(1, d)
    TM, TT, TM_DEC, BB = 512, 512, n_dec, SUBLANES

    xp = x_prompt
    xs = x_sample.reshape(n_dec, d)
    mem2 = mem_prompt.reshape(bsz * N_MEM, d)
    pools_p, convs_p, hs_p, pools_s, convs_s, hs_s = ([] for _ in range(6))
    mem_k, mem_v, mem_kh, mem_vh = _mem_kv(mem2, mem_norm.reshape(DEPTH, 1, d), w_mk, w_mv, TM)
    mem_k = mem_k.reshape(DEPTH, bsz, N_MEM, d)
    mem_v = mem_v.reshape(DEPTH, bsz, N_MEM, d)
    zeros_hist = jnp.zeros((bsz, POOL_PAD, D_POOL), F32)
    zeros_conv = jnp.zeros((bsz, CONV_PAD, D_RNN), F32)
    zeros_h = jnp.zeros((bsz, 1, D_RNN), F32)

    for l in range(DEPTH):
        lw = _layer_weights(l, P)
        last_g = fg if l == DEPTH - 1 else None

        x2, h_last, pool_tail, conv_tail = _mixer_prompt(xp, zeros_hist, zeros_conv, zeros_h, lw,
                                                         0, TT)
        pools_p.append(pool_tail[:, POOL_PAD - POOL_HIST:, :])
        convs_p.append(conv_tail[:, CONV_PAD - (CONV_WIDTH - 1):, :])
        hs_p.append(h_last[:, 0, :])
        x3 = _attn_prompt(x2, mem_k, mem_v, l, lw["norm_mem"], lw["w_mq"], lw["w_mo"], TT)
        xp = _channel_mixer(x3.reshape(n_tok, d), l, lw, P, TM, last_g, True)
        xp = xp.reshape(bsz, t_len, d)

        proj_s = _norm_matmul(xs, lw["norm_mix"], lw["w_in"], TM_DEC)
        mixed_s, h_new = _mixer_decode(proj_s, jnp.swapaxes(state_pool[l], 0, 1),
                                       jnp.swapaxes(state_conv[l], 0, 1), state_h[l], lw,
                                       PAST_LEN)
        pools_s.append(jnp.concatenate([state_pool[l][:, 1:], proj_s[:, None, :D_POOL]], axis=1))
        convs_s.append(jnp.concatenate([state_conv[l][:, 1:], proj_s[:, None, D_POOL:D_MIX]],
                                       axis=1))
        hs_s.append(h_new)
        xs2 = _matmul_res(mixed_s, lw["w_out"], xs, TM_DEC)
        q_s = _norm_matmul(xs2, lw["norm_mem"], lw["w_mq"], TM_DEC)
        att = _attn_decode(q_s.reshape(n_dec, MEM_HEADS, MEM_HEAD_DIM), cache_mem_k, cache_mem_v,
                           l, BB)
        xs3 = _matmul_res(att.reshape(n_dec, d), lw["w_mo"], xs2, TM_DEC)
        xs = _channel_mixer(xs3, l, lw, P, TM_DEC, last_g, False)

    kv_shape = (DEPTH, bsz, N_MEM, MEM_HEADS, MEM_HEAD_DIM)
    return (xp, xs.reshape(n_dec, 1, d), jnp.stack(pools_p), jnp.stack(convs_p), jnp.stack(hs_p),
            mem_kh.reshape(kv_shape), mem_vh.reshape(kv_shape), jnp.stack(pools_s),
            jnp.stack(convs_s), jnp.stack(hs_s))
```

```python
import functools
import math

import jax
import jax.numpy as jnp
from jax import lax
from jax.experimental import pallas as pl
from jax.experimental.pallas import tpu as pltpu

F32 = jnp.float32
BF16 = jnp.bfloat16

D_MODEL = 1024
DEPTH = 2
PAST_LEN = 16384
D_POOL = 512
POOL_WINDOWS = (2, 4, 8, 16)
POOL_GROUP = 128
POOL_HIST = 15
D_RNN = 512
RNN_HEADS = 8
CONV_WIDTH = 4
RGLRU_C = 8.0
D_MIX = 1024
D_IN = 1536
N_MEM = 256
MEM_HEADS = 4
MEM_HEAD_DIM = 256
N_EXPERTS = 8
D_FF_CHUNK = 1408
EPS = 1e-6

SUBLANES = 8
LANES = 128
POOL_PAD = 16
CONV_PAD = 8
VMEM_LIMIT = 56 * 1024 * 1024
SLOT_MAP_CHUNK = 8192
DISPATCH_CHUNK = 2048
TOK_BITS = 14
MOE_CHUNK = 256
MOE_TILE = 576


def _cparams(sem):
    return pltpu.CompilerParams(dimension_semantics=sem, vmem_limit_bytes=VMEM_LIMIT)


def _rms(x, g):
    ms = jnp.mean(x * x, axis=-1, keepdims=True)
    return x * lax.rsqrt(ms + EPS) * g


def _gelu_tanh(x):
    c = math.sqrt(2.0 / math.pi)
    return 0.5 * x * (1.0 + jnp.tanh(c * (x + 0.044715 * (x * x * x))))


def _softplus(x):
    return jnp.maximum(x, 0.0) + jnp.log1p(jnp.exp(-jnp.abs(x)))


def _norm_matmul_body(x_ref, g_ref, w_ref, o_ref):
    hn = _rms(x_ref[...], g_ref[...]).astype(BF16)
    o_ref[...] = jnp.dot(hn, w_ref[...], preferred_element_type=F32)


def _norm_matmul(x, g, w, tm):
    n, d = x.shape
    m = w.shape[1]
    return pl.pallas_call(
        _norm_matmul_body,
        grid=(n // tm,),
        in_specs=[pl.BlockSpec((tm, d), lambda i: (i, 0)),
                  pl.BlockSpec((1, d), lambda i: (0, 0)),
                  pl.BlockSpec((d, m), lambda i: (0, 0))],
        out_specs=pl.BlockSpec((tm, m), lambda i: (i, 0)),
        out_shape=jax.ShapeDtypeStruct((n, m), F32),
        compiler_params=_cparams(("parallel",)),
        name="norm_matmul",
    )(x, g, w)


def _mem_kv_body(x_ref, g_ref, wk_ref, wv_ref, k_ref, v_ref, kh_ref, vh_ref):
    hn = _rms(x_ref[...], g_ref[...]).astype(BF16)
    for w_ref, flat_ref, heads_ref in ((wk_ref, k_ref, kh_ref), (wv_ref, v_ref, vh_ref)):
        y = jnp.dot(hn, w_ref[...].astype(BF16), preferred_element_type=F32)
        flat_ref[...] = y
        heads_ref[...] = y.reshape(heads_ref.shape)


def _mem_kv(mem, g, wk, wv, tm):
    n, d = mem.shape
    n_layers = wk.shape[0]
    w_spec = pl.BlockSpec((None, d, d), lambda l, i: (l, 0, 0))
    flat = pl.BlockSpec((None, tm, d), lambda l, i: (l, i, 0))
    heads = pl.BlockSpec((None, tm, MEM_HEADS, MEM_HEAD_DIM), lambda l, i: (l, i, 0, 0))
    return pl.pallas_call(
        _mem_kv_body,
        grid=(n_layers, n // tm),
        in_specs=[pl.BlockSpec((tm, d), lambda l, i: (i, 0)),
                  pl.BlockSpec((None, 1, d), lambda l, i: (l, 0, 0)), w_spec, w_spec],
        out_specs=[flat, flat, heads, heads],
        out_shape=[jax.ShapeDtypeStruct((n_layers, n, d), F32)] * 2
        + [jax.ShapeDtypeStruct((n_layers, n, MEM_HEADS, MEM_HEAD_DIM), F32)] * 2,
        compiler_params=_cparams(("parallel", "parallel")),
        name="mem_kv",
    )(mem, g, wk, wv)


def _matmul_res_body(a_ref, w_ref, r_ref, o_ref):
    o_ref[...] = r_ref[...] + jnp.dot(a_ref[...].astype(BF16), w_ref[...],
                                      preferred_element_type=F32)


def _matmul_res(a, w, res, tm):
    n, k = a.shape
    m = w.shape[1]
    return pl.pallas_call(
        _matmul_res_body,
        grid=(n // tm,),
        in_specs=[pl.BlockSpec((tm, k), lambda i: (i, 0)),
                  pl.BlockSpec((k, m), lambda i: (0, 0)),
                  pl.BlockSpec((tm, m), lambda i: (i, 0))],
        out_specs=pl.BlockSpec((tm, m), lambda i: (i, 0)),
        out_shape=jax.ShapeDtypeStruct((n, m), F32),
        compiler_params=_cparams(("parallel",)),
        name="matmul_res",
    )(a, w, res)


def _pool_project(z_groups, pw_ref, scale):
    ys = [jnp.dot(z.astype(BF16), pw_ref[g], preferred_element_type=F32)
          for g, z in enumerate(z_groups)]
    return jnp.concatenate(ys, axis=-1) * scale


def _rglru_terms(c, first_pos, wa_ref, ba_ref, wx_ref, bx_ref, lam_ref):
    cb = c.astype(BF16)
    r = jax.nn.sigmoid(jnp.dot(cb, wa_ref[...], preferred_element_type=F32) + ba_ref[...])
    i = jax.nn.sigmoid(jnp.dot(cb, wx_ref[...], preferred_element_type=F32) + bx_ref[...])
    log_a = (-RGLRU_C) * r * _softplus(-lam_ref[...])
    a = jnp.exp(log_a)
    mult = jnp.sqrt(1.0 - a * a)
    if first_pos is not None:
        mult = jnp.where(first_pos, 1.0, mult)
    return a, mult * (i * c)


def _mix_out(y_pool, h, u_gate, gp_ref, gr_ref):
    y_rnn = h * _gelu_tanh(u_gate)
    return jnp.concatenate([_rms(y_pool, gp_ref[...]), _rms(y_rnn, gr_ref[...])], axis=-1)


def _mixer_body(pos0, tt, x_ref, gm_ref, win_ref, wout_ref, hist_ref, conv_ref, h0_ref,
                pw_ref, ps_ref, cw_ref, cb_ref, wa_ref, ba_ref, wx_ref, bx_ref, lam_ref, gp_ref,
                gr_ref, o_ref, hl_ref, ptail_ref, ctail_ref, proj_ref, pbuf, cbuf, a_s, b_s, hc):
    t = pl.program_id(1)
    hn = _rms(x_ref[...], gm_ref[...]).astype(BF16)
    proj_ref[...] = jnp.dot(hn, win_ref[...], preferred_element_type=F32)

    @pl.when(t == 0)
    def _():
        pbuf[0:POOL_PAD, :] = hist_ref[...]
        cbuf[0:CONV_PAD, :] = conv_ref[...]
        hc[...] = jnp.broadcast_to(h0_ref[...], (SUBLANES, D_RNN))

    @pl.when(t > 0)
    def _():
        pbuf[0:POOL_PAD, :] = pbuf[tt:tt + POOL_PAD, :]
        cbuf[0:CONV_PAD, :] = cbuf[tt:tt + CONV_PAD, :]

    u_pool = proj_ref[:, 0:D_POOL]
    pbuf[POOL_PAD:POOL_PAD + tt, :] = u_pool
    cbuf[CONV_PAD:CONV_PAD + tt, :] = proj_ref[:, D_POOL:D_MIX]

    pos = pos0 + t * tt + lax.broadcasted_iota(jnp.int32, (tt, 1), 0)

    zs = []
    for g, w in enumerate(POOL_WINDOWS):
        sl = slice(g * POOL_GROUP, (g + 1) * POOL_GROUP)
        acc = pbuf[POOL_PAD:POOL_PAD + tt, sl]
        for k in range(1, w):
            acc = acc + pbuf[POOL_PAD - k:POOL_PAD - k + tt, sl]
        cnt = jnp.minimum(pos + 1, w).astype(F32)
        zs.append(acc / cnt - u_pool[:, sl])
    y_pool = _pool_project(zs, pw_ref, ps_ref[...])

    c = cb_ref[...]
    for k in range(CONV_WIDTH):
        off = CONV_PAD - (CONV_WIDTH - 1) + k
        c = c + cbuf[off:off + tt, :] * cw_ref[k:k + 1, :]

    a, b = _rglru_terms(c, pos == 0, wa_ref, ba_ref, wx_ref, bx_ref, lam_ref)
    a_s[...] = a
    b_s[...] = b

    row = lax.broadcasted_iota(jnp.int32, (SUBLANES, D_RNN), 0)

    def group(j, carry):
        r0 = pl.multiple_of(j * SUBLANES, SUBLANES)
        ga = a_s[pl.ds(r0, SUBLANES), :]
        gb = b_s[pl.ds(r0, SUBLANES), :]
        for s in (1, 2, 4):
            keep = row >= s
            gb = jnp.where(keep, ga * pltpu.roll(gb, s, axis=0) + gb, gb)
            ga = jnp.where(keep, ga * pltpu.roll(ga, s, axis=0), ga)
        h = ga * carry + gb
        b_s[pl.ds(r0, SUBLANES), :] = h
        return jnp.broadcast_to(h[SUBLANES - 1:SUBLANES, :], (SUBLANES, D_RNN))

    carry = lax.fori_loop(0, tt // SUBLANES, group, hc[...], unroll=2)
    hc[...] = carry
    hl_ref[...] = carry

    mix = _mix_out(y_pool, b_s[...], proj_ref[:, D_MIX:D_IN], gp_ref, gr_ref)
    o_ref[...] = x_ref[...] + jnp.dot(mix.astype(BF16), wout_ref[...],
                                      preferred_element_type=F32)
    ptail_ref[...] = pbuf[tt:tt + POOL_PAD, :]
    ctail_ref[...] = cbuf[tt:tt + CONV_PAD, :]


def _mixer_prompt(x, hist, conv, h0, lw, pos0, tt):
    bsz, t_len, d = x.shape
    full = lambda shape: pl.BlockSpec(shape, lambda b, t: (0,) * len(shape))
    per_b = lambda rows, cols: pl.BlockSpec((None, rows, cols), lambda b, t: (b, 0, 0))
    xs = pl.BlockSpec((None, tt, d), lambda b, t: (b, t, 0))
    return pl.pallas_call(
        functools.partial(_mixer_body, pos0, tt),
        grid=(bsz, t_len // tt),
        in_specs=[xs, full((1, d)), full((d, D_IN)), full((D_MIX, d)),
                  per_b(POOL_PAD, D_POOL), per_b(CONV_PAD, D_RNN), per_b(1, D_RNN),
                  full((len(POOL_WINDOWS), POOL_GROUP, POOL_GROUP)), full((1, D_POOL)),
                  full((CONV_WIDTH, D_RNN)), full((1, D_RNN)),
                  full((D_RNN, D_RNN)), full((1, D_RNN)),
                  full((D_RNN, D_RNN)), full((1, D_RNN)),
                  full((1, D_RNN)), full((1, D_POOL)), full((1, D_RNN))],
        out_specs=[xs, per_b(SUBLANES, D_RNN), per_b(POOL_PAD, D_POOL), per_b(CONV_PAD, D_RNN)],
        out_shape=[jax.ShapeDtypeStruct((bsz, t_len, d), F32),
                   jax.ShapeDtypeStruct((bsz, SUBLANES, D_RNN), F32),
                   jax.ShapeDtypeStruct((bsz, POOL_PAD, D_POOL), F32),
                   jax.ShapeDtypeStruct((bsz, CONV_PAD, D_RNN), F32)],
        scratch_shapes=[pltpu.VMEM((tt, D_IN), F32),
                        pltpu.VMEM((POOL_PAD + tt, D_POOL), F32),
                        pltpu.VMEM((CONV_PAD + tt, D_RNN), F32),
                        pltpu.VMEM((tt, D_RNN), F32),
                        pltpu.VMEM((tt, D_RNN), F32),
                        pltpu.VMEM((SUBLANES, D_RNN), F32)],
        compiler_params=_cparams(("parallel", "arbitrary")),
        name="mixer_prompt",
    )(x, lw["norm_mix"], lw["w_in"], lw["w_out"], hist, conv, h0,
      lw["pool_w"], lw["pool_scale"], lw["conv_w"], lw["conv_b"],
      lw["gate_a_w"], lw["gate_a_b"], lw["gate_x_w"], lw["gate_x_b"], lw["lam"],
      lw["out_norm_pool"], lw["out_norm_rnn"])


def _mixer_decode_body(pos0, proj_ref, hist_ref, conv_ref, h0_ref, pw_ref, ps_ref, cw_ref, cb_ref,
                       wa_ref, ba_ref, wx_ref, bx_ref, lam_ref, gp_ref, gr_ref, mix_ref, hn_ref):
    u_pool = proj_ref[:, 0:D_POOL]
    zs = []
    for g, w in enumerate(POOL_WINDOWS):
        sl = slice(g * POOL_GROUP, (g + 1) * POOL_GROUP)
        acc = u_pool[:, sl]
        for k in range(1, w):
            acc = acc + hist_ref[POOL_HIST - k, :, sl]
        zs.append(acc / float(min(pos0 + 1, w)) - u_pool[:, sl])
    y_pool = _pool_project(zs, pw_ref, ps_ref[...])

    c = cb_ref[...]
    for k in range(CONV_WIDTH - 1):
        c = c + conv_ref[k] * cw_ref[k:k + 1, :]
    c = c + proj_ref[:, D_POOL:D_MIX] * cw_ref[CONV_WIDTH - 1:CONV_WIDTH, :]

    a, b = _rglru_terms(c, None, wa_ref, ba_ref, wx_ref, bx_ref, lam_ref)
    h = a * h0_ref[...] + b
    hn_ref[...] = h
    mix_ref[...] = _mix_out(y_pool, h, proj_ref[:, D_MIX:D_IN], gp_ref, gr_ref).astype(BF16)


def _mixer_decode(proj, hist_t, conv_t, h0, lw, pos0):
    assert pos0 > 0
    n = proj.shape[0]
    vm = pl.BlockSpec(memory_space=pltpu.VMEM)
    return pl.pallas_call(
        functools.partial(_mixer_decode_body, pos0),
        in_specs=[vm] * 15,
        out_specs=[vm, vm],
        out_shape=[jax.ShapeDtypeStruct((n, D_MIX), BF16),
                   jax.ShapeDtypeStruct((n, D_RNN), F32)],
        compiler_params=pltpu.CompilerParams(vmem_limit_bytes=VMEM_LIMIT),
        name="mixer_decode",
    )(proj, hist_t, conv_t, h0, lw["pool_w"], lw["pool_scale"], lw["conv_w"], lw["conv_b"],
      lw["gate_a_w"], lw["gate_a_b"], lw["gate_x_w"], lw["gate_x_b"], lw["lam"],
      lw["out_norm_pool"], lw["out_norm_rnn"])


def _softmax_rows(s):
    e = jnp.exp(s - jnp.max(s, axis=-1, keepdims=True))
    return e / jnp.sum(e, axis=-1, keepdims=True)


_NT = (((1,), (1,)), ((), ()))


def _attn_prompt_body(x_ref, k_ref, v_ref, g_ref, wq_ref, wo_ref, o_ref):
    x = x_ref[...]
    xn = _rms(x, g_ref[...]).astype(BF16)
    q = jnp.dot(xn, wq_ref[...], preferred_element_type=F32)
    outs = []
    for h in range(MEM_HEADS):
        sl = slice(h * MEM_HEAD_DIM, (h + 1) * MEM_HEAD_DIM)
        s = lax.dot_general(q[:, sl].astype(BF16), k_ref[:, sl].astype(BF16), _NT,
                            preferred_element_type=F32) * (MEM_HEAD_DIM ** -0.5)
        p = _softmax_rows(s).astype(BF16)
        outs.append(jnp.dot(p, v_ref[:, sl].astype(BF16), preferred_element_type=F32))
    o = jnp.concatenate(outs, axis=-1).astype(BF16)
    o_ref[...] = x + jnp.dot(o, wo_ref[...], preferred_element_type=F32)


def _attn_prompt(x, k, v, layer, g, wq, wo, tt):
    bsz, t_len, d = x.shape
    full = lambda shape: pl.BlockSpec(shape, lambda b, t: (0,) * len(shape))
    kv = pl.BlockSpec((None, None, N_MEM, d), lambda b, t: (layer, b, 0, 0))
    xs = pl.BlockSpec((None, tt, d), lambda b, t: (b, t, 0))
    return pl.pallas_call(
        _attn_prompt_body,
        grid=(bsz, t_len // tt),
        in_specs=[xs, kv, kv, full((1, d)), full((d, d)), full((d, d))],
        out_specs=xs,
        out_shape=jax.ShapeDtypeStruct(x.shape, F32),
        compiler_params=_cparams(("parallel", "parallel")),
        name="attn_prompt",
    )(x, k, v, g, wq, wo)


def _attn_decode_body(bb, q_ref, k_ref, v_ref, o_ref):
    for b in range(bb):
        k = k_ref[b]
        q = q_ref[b] * (MEM_HEAD_DIM ** -0.5)
        s = jnp.broadcast_to(jnp.sum(k * q[None], axis=-1, keepdims=True), k.shape)
        e = jnp.exp(s - jnp.max(s, axis=0, keepdims=True))
        o_ref[b] = jnp.sum(e * v_ref[b], axis=0) / jnp.sum(e, axis=0)


def _attn_decode(q, k, v, layer, bb):
    n, nh, hd = q.shape
    kv = pl.BlockSpec((None, bb, N_MEM, nh, hd), lambda i: (layer, i, 0, 0, 0))
    qs = pl.BlockSpec((bb, nh, hd), lambda i: (i, 0, 0))
    return pl.pallas_call(
        functools.partial(_attn_decode_body, bb),
        grid=(n // bb,),
        in_specs=[qs, kv, kv],
        out_specs=qs,
        out_shape=jax.ShapeDtypeStruct((n, nh, hd), F32),
        compiler_params=_cparams(("parallel",)),
        name="attn_decode",
    )(q, k, v)


def _top2(hn, rw_ref):
    rw = rw_ref[...]
    hn_hi, rw_hi = hn.astype(BF16), rw.astype(BF16)
    hn_lo = (hn - hn_hi.astype(F32)).astype(BF16)
    rw_lo = (rw - rw_hi.astype(F32)).astype(BF16)
    logits = (jnp.dot(hn_hi, rw_hi, preferred_element_type=F32)
              + jnp.dot(hn_lo, rw_hi, preferred_element_type=F32)
              + jnp.dot(hn_hi, rw_lo, preferred_element_type=F32))
    lane = lax.broadcasted_iota(jnp.int32, logits.shape, 1)
    neg = -jnp.inf
    lg = jnp.where(lane < N_EXPERTS, logits, neg)
    m1 = jnp.max(lg, axis=-1, keepdims=True)
    i1 = jnp.min(jnp.where(lg == m1, lane, LANES), axis=-1, keepdims=True)
    lg2 = jnp.where(lane == i1, neg, lg)
    m2 = jnp.max(lg2, axis=-1, keepdims=True)
    i2 = jnp.min(jnp.where(lg2 == m2, lane, LANES), axis=-1, keepdims=True)
    e2 = jnp.exp(m2 - m1)
    den = 1.0 + e2
    return lane, i1, i2, 1.0 / den, e2 / den


def _router_body(x_ref, g_ref, rw_ref, cw_ref):
    lane, i1, i2, g1, g2 = _top2(_rms(x_ref[...], g_ref[...]), rw_ref)
    cw_ref[...] = jnp.where(lane == i1, g1, 0.0) + jnp.where(lane == i2, g2, 0.0)


def _router(x, g, rw, tm):
    n, d = x.shape
    return pl.pallas_call(
        _router_body,
        grid=(n // tm,),
        in_specs=[pl.BlockSpec((tm, d), lambda i: (i, 0)),
                  pl.BlockSpec((1, d), lambda i: (0, 0)),
                  pl.BlockSpec((d, LANES), lambda i: (0, 0))],
        out_specs=pl.BlockSpec((tm, LANES), lambda i: (i, 0)),
        out_shape=jax.ShapeDtypeStruct((n, LANES), F32),
        compiler_params=_cparams(("parallel",)),
        name="router",
    )(x, g, rw)


def _store_row_tiles(ref, val):
    rows = val.shape[0]
    for c in range(val.shape[1] // LANES):
        ref[pl.ds(c, rows, stride=SUBLANES), :] = val[:, c * LANES:(c + 1) * LANES]


def _load_row_tiles(ref, rows):
    return jnp.concatenate([ref[pl.ds(c, rows, stride=SUBLANES), :] for c in range(SUBLANES)],
                           axis=-1)


R_E1, R_E2, R_RANK1, R_RANK2, R_G1, R_G2 = range(6)


def _router_rank_body(x_ref, g_ref, rw_ref, rec_ref, rows_ref, cnt_ref, hn_ref, carry):
    i = pl.program_id(0)
    tm = x_ref.shape[0]

    @pl.when(i == 0)
    def _():
        carry[...] = jnp.zeros_like(carry)

    hn = _rms(x_ref[...], g_ref[...])
    _store_row_tiles(hn_ref, hn)
    lane, i1, i2, g1, g2 = _top2(hn, rw_ref)
    picks = ((lane == i1) | (lane == i2)).astype(F32)
    r_io = lax.broadcasted_iota(jnp.int32, (tm, tm), 0)
    c_io = lax.broadcasted_iota(jnp.int32, (tm, tm), 1)
    earlier = (c_io < r_io).astype(BF16)
    before = jnp.dot(earlier, picks.astype(BF16), preferred_element_type=F32) + carry[...]
    rank1 = jnp.sum(jnp.where(lane == i1, before, 0.0), axis=-1, keepdims=True)
    rank2 = jnp.sum(jnp.where(lane == i2, before, 0.0), axis=-1, keepdims=True)
    carry[...] += jnp.sum(picks, axis=0, keepdims=True)
    cnt_ref[...] = jnp.broadcast_to(carry[...], cnt_ref.shape)

    rec = jnp.zeros((tm, LANES), F32)
    for k, v in ((R_E1, i1.astype(F32)), (R_E2, i2.astype(F32)), (R_RANK1, rank1),
                 (R_RANK2, rank2), (R_G1, g1), (R_G2, g2)):
        rec = jnp.where(lane == k, v, rec)
    rec_ref[...] = rec
    rows_ref[...] = rec.T[0:SUBLANES, :].astype(jnp.int32)


def _router_rank(x, g, rw, tm):
    n, d = x.shape
    assert d == SUBLANES * LANES
    return pl.pallas_call(
        _router_rank_body,
        grid=(n // tm,),
        in_specs=[pl.BlockSpec((tm, d), lambda i: (i, 0)),
                  pl.BlockSpec((1, d), lambda i: (0, 0)),
                  pl.BlockSpec((d, LANES), lambda i: (0, 0))],
        out_specs=[pl.BlockSpec((tm, LANES), lambda i: (i, 0)),
                   pl.BlockSpec((SUBLANES, tm), lambda i: (0, i)),
                   pl.BlockSpec((SUBLANES, LANES), lambda i: (0, 0)),
                   pl.BlockSpec((tm * SUBLANES, LANES), lambda i: (i, 0))],
        out_shape=[jax.ShapeDtypeStruct((n, LANES), F32),
                   jax.ShapeDtypeStruct((SUBLANES, n), jnp.int32),
                   jax.ShapeDtypeStruct((SUBLANES, LANES), F32),
                   jax.ShapeDtypeStruct((n * SUBLANES, LANES), F32)],
        scratch_shapes=[pltpu.VMEM((1, LANES), F32)],
        compiler_params=_cparams(("arbitrary",)),
        name="router_rank",
    )(x, g, rw)


def _slot_map_body(n_tok, tm, n_tiles, cnt_ref, rows_ref, slot_ref, te_ref, nu_ref, off_s,
                   pos_v, pos1_s, pos2_s):
    c = pl.program_id(0)
    tc = rows_ref.shape[1]

    def fill_pad(s, ordinal):
        slot_ref[s] = (2 * n_tok + ordinal) << TOK_BITS
        return ordinal + 1

    @pl.when(c == 0)
    def _():
        lead = lax.fori_loop(0, tm, fill_pad, jnp.int32(0))

        def per_expert(e, carry):
            tile0, ordinal = carry
            n = cnt_ref[e]
            start = (tile0 + 1) * tm
            off_s[e] = start
            tile1 = tile0 + lax.div(n + (tm - 1), tm)

            def fill_tile(t, z):
                te_ref[t] = e
                return z

            lax.fori_loop(tile0, tile1, fill_tile, 0)
            ordinal = lax.fori_loop(start + n, (tile1 + 1) * tm, fill_pad, ordinal)
            return tile1, ordinal

        used, ordinal = lax.fori_loop(0, N_EXPERTS, per_expert, (jnp.int32(0), lead))
        nu_ref[0] = used

        def fill_rest(t, z):
            te_ref[t] = N_EXPERTS - 1
            return z

        lax.fori_loop(used, n_tiles, fill_rest, 0)
        lax.fori_loop((used + 1) * tm, (n_tiles + 2) * tm, fill_pad, ordinal)

    rows = rows_ref[...]
    pos = rows[R_RANK1:R_RANK2 + 1, :]
    for e in range(N_EXPERTS):
        pos = pos + jnp.where(rows[R_E1:R_E2 + 1, :] == e, off_s[e], 0)
    pos_v[...] = pos
    pltpu.sync_copy(pos_v.at[0], pos1_s)
    pltpu.sync_copy(pos_v.at[1], pos2_s)

    step = (1 << TOK_BITS) + 1

    def per_token(j, first):
        slot_ref[pos1_s[j]] = first
        slot_ref[pos2_s[j]] = first + (n_tok << TOK_BITS)
        return first + step

    lax.fori_loop(0, tc, per_token, (c * tc) * step, unroll=8)


def _slot_map(counts, rows, n_tok, tm, n_tiles, tc):
    smem = pl.BlockSpec(memory_space=pltpu.SMEM)
    return pl.pallas_call(
        functools.partial(_slot_map_body, n_tok, tm, n_tiles),
        grid=(n_tok // tc,),
        in_specs=[smem, pl.BlockSpec((SUBLANES, tc), lambda c: (0, c))],
        out_specs=[smem, smem, smem],
        out_shape=[jax.ShapeDtypeStruct(((n_tiles + 2) * tm,), jnp.int32),
                   jax.ShapeDtypeStruct((n_tiles,), jnp.int32),
                   jax.ShapeDtypeStruct((1,), jnp.int32)],
        scratch_shapes=[pltpu.SMEM((N_EXPERTS,), jnp.int32), pltpu.VMEM((2, tc), jnp.int32),
                        pltpu.SMEM((tc,), jnp.int32), pltpu.SMEM((tc,), jnp.int32)],
        compiler_params=_cparams(("arbitrary",)),
        name="slot_map",
    )(counts, rows)


def _experts_body(tm, slot_ref, te_ref, nu_ref, x_hbm, wg_ref, wu_ref, wd_ref,
                  out_hbm, xbuf, ybuf, hn_s, mid_s, acc, gsem, ssem, zsem):
    del te_ref
    i = pl.program_id(0)
    n_used = nu_ref[0]
    cur = i % 2
    nxt = 1 - cur

    def tile_rows(r):
        return pl.ds(pl.multiple_of(r * SUBLANES, SUBLANES), SUBLANES)

    def gather_copy(src_row, r, buf):
        return pltpu.make_async_copy(x_hbm.at[tile_rows(src_row), :],
                                     xbuf.at[buf, tile_rows(r), :], gsem.at[buf])

    def scatter_copy(dst_row, r, buf):
        return pltpu.make_async_copy(ybuf.at[buf, tile_rows(r), :],
                                     out_hbm.at[tile_rows(dst_row), :], ssem.at[buf])

    def gather_src(tile, r):
        return slot_ref[(tile + 1) * tm + r] & ((1 << TOK_BITS) - 1)

    def scatter_dst(tile, r):
        return slot_ref[(tile + 1) * tm + r] >> TOK_BITS

    def wait_gathers(buf):
        for r in range(tm):
            gather_copy(0, r, buf).wait()

    def wait_scatters(buf):
        for r in range(tm):
            scatter_copy(0, r, buf).wait()

    @pl.when(i == 0)
    def _():
        ybuf[...] = jnp.zeros_like(ybuf)

        def zero_tile(j, z):
            rows = pl.ds(pl.multiple_of(j * (tm * SUBLANES), SUBLANES), tm * SUBLANES)
            cp = pltpu.make_async_copy(ybuf.at[0], out_hbm.at[rows, :], zsem.at[0])
            cp.start()
            cp.wait()
            return z

        lax.fori_loop(n_used + 1, out_hbm.shape[0] // (tm * SUBLANES), zero_tile, 0)
        for r in range(tm):
            gather_copy(gather_src(0, r), r, 0).start()

    @pl.when(i < n_used)
    def _():
        wait_gathers(cur)
        hn_s[...] = _load_row_tiles(xbuf.at[cur], tm).astype(BF16)
        acc[...] = jnp.zeros_like(acc)
        mid_s[...] = jnp.zeros_like(mid_s)
        n_chunks = wg_ref.shape[0]
        rows_per = tm // n_chunks

        def chunk(k, carry):
            row0 = pl.multiple_of(k * rows_per, SUBLANES)
            gbase = (i + 2) * tm + row0
            sbase = i * tm + row0
            for r in range(rows_per):
                src = slot_ref[gbase + r] & ((1 << TOK_BITS) - 1)
                gather_copy(src, row0 + r, nxt).start()
                scatter_copy(slot_ref[sbase + r] >> TOK_BITS, row0 + r, nxt).start()
            acc[...] += jnp.dot(mid_s[...], wd_ref[jnp.maximum(k - 1, 0)],
                                preferred_element_type=F32)
            hn = hn_s[...]
            gate = jnp.dot(hn, wg_ref[k], preferred_element_type=F32)
            up = jnp.dot(hn, wu_ref[k], preferred_element_type=F32)
            mid_s[...] = (gate * jax.nn.sigmoid(gate) * up).astype(BF16)
            return carry

        lax.fori_loop(0, n_chunks, chunk, 0)
        y = acc[...] + jnp.dot(mid_s[...], wd_ref[n_chunks - 1], preferred_element_type=F32)

        @pl.when(i >= 1)
        def _():
            wait_scatters(cur)

        _store_row_tiles(ybuf.at[cur], y)

    @pl.when(i == n_used)
    def _():
        wait_gathers(cur)
        wait_scatters(cur)
        for r in range(tm):
            scatter_copy(scatter_dst(i - 1, r), r, nxt).start()
        wait_scatters(nxt)


def _experts_sparse(xt, wg, wu, wd, slot, tile_expert, n_used, tm, n_tiles):
    n_tok = xt.shape[0] // SUBLANES
    _, n_chunks, d, fc = wg.shape
    n_pad = (n_tiles + 2) * tm - 2 * n_tok
    assert n_tok <= 1 << TOK_BITS and 2 * n_tok + n_pad <= 1 << (31 - TOK_BITS)
    assert tm % (n_chunks * SUBLANES) == 0
    w_idx = lambda i, slot, te, nu: (te[jnp.minimum(i, n_tiles - 1)], 0, 0, 0)
    grid_spec = pltpu.PrefetchScalarGridSpec(
        num_scalar_prefetch=3,
        grid=(n_tiles + 1,),
        in_specs=[pl.BlockSpec(memory_space=pl.ANY),
                  pl.BlockSpec((None, n_chunks, d, fc), w_idx),
                  pl.BlockSpec((None, n_chunks, d, fc), w_idx),
                  pl.BlockSpec((None, n_chunks, fc, d), w_idx)],
        out_specs=pl.BlockSpec(memory_space=pl.ANY),
        scratch_shapes=[pltpu.VMEM((2, tm * SUBLANES, LANES), F32),
                        pltpu.VMEM((2, tm * SUBLANES, LANES), F32),
                        pltpu.VMEM((tm, d), BF16), pltpu.VMEM((tm, fc), BF16),
                        pltpu.VMEM((tm, d), F32),
                        pltpu.SemaphoreType.DMA((2,)), pltpu.SemaphoreType.DMA((2,)),
                        pltpu.SemaphoreType.DMA((1,))])
    return pl.pallas_call(
        functools.partial(_experts_body, tm),
        grid_spec=grid_spec,
        out_shape=jax.ShapeDtypeStruct(((2 * n_tok + n_pad) * SUBLANES, LANES), F32),
        compiler_params=_cparams(("arbitrary",)),
        name="experts_sparse",
    )(slot, tile_expert, n_used, xt, wg, wu, wd)


def _combine_body(final, x_ref, y1_ref, y2_ref, rec_ref, *rest):
    o_ref = rest[-1]
    tm = x_ref.shape[0]
    rec = rec_ref[...]
    out = (x_ref[...] + rec[:, R_G1:R_G1 + 1] * _load_row_tiles(y1_ref, tm)
           + rec[:, R_G2:R_G2 + 1] * _load_row_tiles(y2_ref, tm))
    o_ref[...] = _rms(out, rest[0][...]) if final else out


def _combine(x, y, rec, tm, final_g=None):
    n, d = x.shape
    final = final_g is not None
    nb = n // tm
    row = pl.BlockSpec((tm, d), lambda i: (i, 0))
    in_specs = [row, pl.BlockSpec((tm * SUBLANES, LANES), lambda i: (i, 0)),
                pl.BlockSpec((tm * SUBLANES, LANES), lambda i: (nb + i, 0)),
                pl.BlockSpec((tm, LANES), lambda i: (i, 0))]
    args = [x, y, y, rec]
    if final:
        in_specs.append(pl.BlockSpec((1, d), lambda i: (0, 0)))
        args.append(final_g)
    return pl.pallas_call(
        functools.partial(_combine_body, final),
        grid=(n // tm,),
        in_specs=in_specs,
        out_specs=row,
        out_shape=jax.ShapeDtypeStruct((n, d), F32),
        compiler_params=_cparams(("parallel",)),
        name="moe_combine",
    )(*args)


def _tile_rows(r):
    return pl.ds(pl.multiple_of(r * SUBLANES, SUBLANES), SUBLANES)


def _route_plan_body(tm, n_tiles, cnt_ref, rows_ref, pos_ref, te_ref, nu_ref, last_ref, off_s):
    c = pl.program_id(0)

    @pl.when(c == 0)
    def _():
        def per_expert(e, tile0):
            n = cnt_ref[e]
            off_s[e] = tile0 * tm
            tile1 = tile0 + lax.div(n + (tm - 1), tm)

            def fill_tile(t, z):
                te_ref[t] = e
                return z

            lax.fori_loop(tile0, tile1, fill_tile, 0)
            last_ref[e] = jnp.where(tile1 > tile0, tile1 - 1, -1)
            return tile1

        used = lax.fori_loop(0, N_EXPERTS, per_expert, jnp.int32(0))
        nu_ref[0] = used

        def fill_rest(t, z):
            te_ref[t] = N_EXPERTS - 1
            return z

        lax.fori_loop(used, n_tiles, fill_rest, 0)

    rows = rows_ref[...]
    pos = rows[R_RANK1:R_RANK2 + 1, :]
    for e in range(N_EXPERTS):
        pos = pos + jnp.where(rows[R_E1:R_E2 + 1, :] == e, off_s[e], 0)
    pos_ref[...] = jnp.zeros_like(pos_ref)
    pos_ref[0:2, :] = pos


def _route_plan(counts, rows, tm, n_tiles, tc):
    n_tok = rows.shape[1]
    smem = pl.BlockSpec(memory_space=pltpu.SMEM)
    blk = pl.BlockSpec((SUBLANES, tc), lambda c: (0, c))
    return pl.pallas_call(
        functools.partial(_route_plan_body, tm, n_tiles),
        grid=(n_tok // tc,),
        in_specs=[smem, blk],
        out_specs=[blk, smem, smem, smem],
        out_shape=[jax.ShapeDtypeStruct((SUBLANES, n_tok), jnp.int32),
                   jax.ShapeDtypeStruct((n_tiles,), jnp.int32),
                   jax.ShapeDtypeStruct((1,), jnp.int32),
                   jax.ShapeDtypeStruct((N_EXPERTS,), jnp.int32)],
        scratch_shapes=[pltpu.SMEM((N_EXPERTS,), jnp.int32)],
        compiler_params=_cparams(("arbitrary",)),
        name="route_plan",
    )(counts, rows)


def _dispatch_body(tm, n_tiles, last_ref, nu_ref, pos_ref, xt_ref, xs_hbm, zbuf, sem, zsem):
    c = pl.program_id(0)
    tc = pos_ref.shape[1]

    def zero_tile(t):
        rows = pl.ds(pl.multiple_of(t * (tm * SUBLANES), SUBLANES), tm * SUBLANES)
        cp = pltpu.make_async_copy(zbuf, xs_hbm.at[rows, :], zsem.at[0])
        cp.start()
        cp.wait()

    @pl.when(c == 0)
    def _():
        zbuf[...] = jnp.zeros_like(zbuf)
        for e in range(N_EXPERTS):
            @pl.when(last_ref[e] >= 0)
            def _():
                zero_tile(last_ref[e])

        def zero_unused(t, z):
            zero_tile(t)
            return z

        lax.fori_loop(nu_ref[0], n_tiles, zero_unused, 0)

    def row_copy(j, p):
        return pltpu.make_async_copy(xt_ref.at[_tile_rows(j), :], xs_hbm.at[_tile_rows(p), :],
                                     sem.at[0])

    def issue(j, z):
        row_copy(j, pos_ref[0, j]).start()
        row_copy(j, pos_ref[1, j]).start()
        return z

    lax.fori_loop(0, tc, issue, 0, unroll=8)

    def drain(j, z):
        row_copy(0, 0).wait()
        row_copy(0, 0).wait()
        return z

    lax.fori_loop(0, tc, drain, 0, unroll=8)


def _dispatch(xt, pos, last, n_used, tm, n_tiles, tc):
    n_tok = xt.shape[0] // SUBLANES
    smem = pl.BlockSpec(memory_space=pltpu.SMEM)
    return pl.pallas_call(
        functools.partial(_dispatch_body, tm, n_tiles),
        grid=(n_tok // tc,),
        in_specs=[smem, smem,
                  pl.BlockSpec((SUBLANES, tc), lambda c: (0, c), memory_space=pltpu.SMEM),
                  pl.BlockSpec((tc * SUBLANES, LANES), lambda c: (c, 0))],
        out_specs=pl.BlockSpec(memory_space=pl.ANY),
        out_shape=jax.ShapeDtypeStruct((n_tiles * tm * SUBLANES, LANES), F32),
        scratch_shapes=[pltpu.VMEM((tm * SUBLANES, LANES), F32),
                        pltpu.SemaphoreType.DMA((1,)), pltpu.SemaphoreType.DMA((1,))],
        compiler_params=_cparams(("arbitrary",)),
        name="moe_dispatch",
    )(last, n_used, pos, xt)


def _grouped_experts_body(tm, te_ref, nu_ref, x_ref, wg_ref, wu_ref, wd_ref, y_ref):
    del te_ref
    i = pl.program_id(0)

    @pl.when(i < nu_ref[0])
    def _():
        hn = _load_row_tiles(x_ref, tm).astype(BF16)
        acc = None
        for k in range(wg_ref.shape[0]):
            gate = jnp.dot(hn, wg_ref[k], preferred_element_type=F32)
            up = jnp.dot(hn, wu_ref[k], preferred_element_type=F32)
            mid = (gate * jax.nn.sigmoid(gate) * up).astype(BF16)
            y = jnp.dot(mid, wd_ref[k], preferred_element_type=F32)
            acc = y if acc is None else acc + y
        _store_row_tiles(y_ref, acc)

    @pl.when(i >= nu_ref[0])
    def _():
        y_ref[...] = jnp.zeros_like(y_ref)


def _grouped_experts(xs, wg, wu, wd, tile_expert, n_used, tm, n_tiles):
    _, n_chunks, d, fc = wg.shape
    w_idx = lambda i, te, nu: (te[i], 0, 0, 0)
    grid_spec = pltpu.PrefetchScalarGridSpec(
        num_scalar_prefetch=2,
        grid=(n_tiles,),
        in_specs=[pl.BlockSpec((tm * SUBLANES, LANES),
                               lambda i, te, nu: (jnp.minimum(i, nu[0] - 1), 0)),
                  pl.BlockSpec((None, n_chunks, d, fc), w_idx),
                  pl.BlockSpec((None, n_chunks, d, fc), w_idx),
                  pl.BlockSpec((None, n_chunks, fc, d), w_idx)],
        out_specs=pl.BlockSpec((tm * SUBLANES, LANES), lambda i, te, nu: (i, 0)))
    return pl.pallas_call(
        functools.partial(_grouped_experts_body, tm),
        grid_spec=grid_spec,
        out_shape=jax.ShapeDtypeStruct(xs.shape, F32),
        compiler_params=_cparams(("arbitrary",)),
        name="grouped_experts",
    )(tile_expert, n_used, xs, wg, wu, wd)


def _gather_combine_body(final, pos_ref, posn_ref, x_ref, rec_ref, y_hbm, *rest):
    o_ref, y1buf, y2buf, sem = rest[-4:]
    i = pl.program_id(0)
    tm = x_ref.shape[0]
    cur = i % 2

    def row_copy(p, r, dst, buf):
        return pltpu.make_async_copy(y_hbm.at[_tile_rows(p), :], dst.at[buf, _tile_rows(r), :],
                                     sem.at[buf])

    def issue(pref, buf):
        def one(j, z):
            row_copy(pref[0, j], j, y1buf, buf).start()
            row_copy(pref[1, j], j, y2buf, buf).start()
            return z

        lax.fori_loop(0, tm, one, 0, unroll=8)

    @pl.when(i == 0)
    def _():
        issue(pos_ref, 0)

    @pl.when(i + 1 < pl.num_programs(0))
    def _():
        issue(posn_ref, 1 - cur)

    def drain(j, z):
        row_copy(0, 0, y1buf, cur).wait()
        row_copy(0, 0, y2buf, cur).wait()
        return z

    lax.fori_loop(0, tm, drain, 0, unroll=8)
    rec = rec_ref[...]
    out = (x_ref[...] + rec[:, R_G1:R_G1 + 1] * _load_row_tiles(y1buf.at[cur], tm)
           + rec[:, R_G2:R_G2 + 1] * _load_row_tiles(y2buf.at[cur], tm))
    o_ref[...] = _rms(out, rest[0][...]) if final else out


def _gather_combine(x, y, rec, pos, tm, final_g=None):
    n, d = x.shape
    final = final_g is not None
    nb = n // tm
    row = pl.BlockSpec((tm, d), lambda i: (i, 0))
    in_specs = [pl.BlockSpec((SUBLANES, tm), lambda i: (0, i), memory_space=pltpu.SMEM),
                pl.BlockSpec((SUBLANES, tm), lambda i: (0, jnp.minimum(i + 1, nb - 1)),
                             memory_space=pltpu.SMEM),
                row, pl.BlockSpec((tm, LANES), lambda i: (i, 0)),
                pl.BlockSpec(memory_space=pl.ANY)]
    args = [pos, pos, x, rec, y]
    if final:
        in_specs.append(pl.BlockSpec((1, d), lambda i: (0, 0)))
        args.append(final_g)
    return pl.pallas_call(
        functools.partial(_gather_combine_body, final),
        grid=(nb,),
        in_specs=in_specs,
        out_specs=row,
        out_shape=jax.ShapeDtypeStruct((n, d), F32),
        scratch_shapes=[pltpu.VMEM((2, tm * SUBLANES, LANES), F32),
                        pltpu.VMEM((2, tm * SUBLANES, LANES), F32),
                        pltpu.SemaphoreType.DMA((2,))],
        compiler_params=_cparams(("arbitrary",)),
        name="moe_gather_combine",
    )(*args)


def _ffn_body(n_chunks, per_expert, final, *refs):
    routed = per_expert is not None
    x_ref, g_ref, wg_ref, wu_ref, wd_ref = refs[:5]
    rest = list(refs[5:])
    cw_ref = rest.pop(0) if routed else None
    fg_ref = rest.pop(0) if final else None
    o_ref, hn_s = rest
    j = pl.program_id(1)

    @pl.when(j == 0)
    def _():
        hn_s[...] = _rms(x_ref[...], g_ref[...]).astype(BF16)
        o_ref[...] = x_ref[...]

    hn = hn_s[...]

    def swiglu(wg, wu, wd):
        gate = jnp.dot(hn, wg, preferred_element_type=F32)
        up = jnp.dot(hn, wu, preferred_element_type=F32)
        mid = (gate * jax.nn.sigmoid(gate) * up).astype(BF16)
        return jnp.dot(mid, wd, preferred_element_type=F32)

    if routed:
        y = swiglu(wg_ref[0], wu_ref[0], wd_ref[0])
        for c in range(1, per_expert):
            y = y + swiglu(wg_ref[c], wu_ref[c], wd_ref[c])
        cw = cw_ref[...]
        lane = lax.broadcasted_iota(jnp.int32, cw.shape, 1)
        y = y * jnp.sum(jnp.where(lane == j, cw, 0.0), axis=-1, keepdims=True)
    else:
        y = swiglu(wg_ref[...], wu_ref[...], wd_ref[...])
    o_ref[...] += y

    if final:
        @pl.when(j == n_chunks - 1)
        def _():
            o_ref[...] = _rms(o_ref[...], fg_ref[...])


def _ffn(x, g, wg, wu, wd, tm, cw=None, final_g=None):
    n, d = x.shape
    routed = cw is not None
    final = final_g is not None
    row = pl.BlockSpec((tm, d), lambda i, j: (i, 0))
    vec = pl.BlockSpec((1, d), lambda i, j: (0, 0))
    if routed:
        n_chunks, per_expert, _, fc = wg.shape
        w_idx = lambda i, j: (j, 0, 0, 0)
        w_in_spec = pl.BlockSpec((None, per_expert, d, fc), w_idx)
        in_specs = [row, vec, w_in_spec, w_in_spec,
                    pl.BlockSpec((None, per_expert, fc, d), w_idx),
                    pl.BlockSpec((tm, LANES), lambda i, j: (i, 0))]
        args = [x, g, wg, wu, wd, cw]
    else:
        fc = D_FF_CHUNK
        per_expert = None
        n_chunks = wd.shape[0] // fc
        w_in_spec = pl.BlockSpec((d, fc), lambda i, j: (0, j))
        in_specs = [row, vec, w_in_spec, w_in_spec, pl.BlockSpec((fc, d), lambda i, j: (j, 0))]
        args = [x, g, wg, wu, wd]
    if final:
        in_specs.append(vec)
        args.append(final_g)
    return pl.pallas_call(
        functools.partial(_ffn_body, n_chunks, per_expert, final),
        grid=(n // tm, n_chunks),
        in_specs=in_specs,
        out_specs=row,
        out_shape=jax.ShapeDtypeStruct((n, d), F32),
        scratch_shapes=[pltpu.VMEM((tm, d), BF16)],
        compiler_params=_cparams(("parallel", "arbitrary")),
        name="ffn",
    )(*args)


def _block_diag(w):
    h, a, b = w.shape
    eye = jnp.eye(h, dtype=w.dtype)
    return (eye[:, None, :, None] * w[:, :, None, :]).reshape(h * a, h * b)


def _layer_weights(l, P):
    row = lambda v: v[l].reshape(1, -1)
    return dict(
        norm_mix=row(P["norm_mix"]), w_in=P["w_in"][l].astype(BF16),
        pool_w=P["pool_w"][l].astype(BF16), pool_scale=row(P["pool_scale"]),
        conv_w=P["conv_w"][l], conv_b=row(P["conv_b"]),
        gate_a_w=_block_diag(P["gate_a_w"][l]).astype(BF16), gate_a_b=row(P["gate_a_b"]),
        gate_x_w=_block_diag(P["gate_x_w"][l]).astype(BF16), gate_x_b=row(P["gate_x_b"]),
        lam=row(P["rglru_lambda"]),
        out_norm_pool=row(P["out_norm_pool"]), out_norm_rnn=row(P["out_norm_rnn"]),
        w_out=P["w_out"][l].astype(BF16), norm_mem=row(P["norm_mem"]),
        w_mq=P["w_mq"][l].astype(BF16), w_mo=P["w_mo"][l].astype(BF16),
        norm_ffn=row(P["norm_ffn"]),
        moe=(_chunk_experts(P["moe_w_gate"][l // 2], P["moe_w_up"][l // 2], P["moe_w_down"][l // 2])
             if l % 2 == 1 else None))


def _channel_mixer(x, l, lw, P, tm, final_g, sparse):
    j = l // 2
    n = x.shape[0]
    if l % 2 == 0:
        return _ffn(x, lw["norm_ffn"], P["ffn_w_gate"][j].astype(BF16),
                    P["ffn_w_up"][j].astype(BF16), P["ffn_w_down"][j].astype(BF16),
                    tm, final_g=final_g)
    rw = jnp.pad(P["router_w"][j], ((0, 0), (0, LANES - N_EXPERTS)))
    wg, wu, wd = lw["moe"]
    if not sparse:
        cw = _router(x, lw["norm_ffn"], rw, tm)
        return _ffn(x, lw["norm_ffn"], wg, wu, wd, tm, cw=cw, final_g=final_g)
    te = MOE_TILE
    n_tiles = -(-2 * n // te) + N_EXPERTS
    rec, rows, cnt, xt = _router_rank(x, lw["norm_ffn"], rw, tm)
    counts = cnt[0, :N_EXPERTS].astype(jnp.int32)
    pos, tile_expert, n_used, last = _route_plan(counts, rows, te, n_tiles, DISPATCH_CHUNK)
    xs = _dispatch(xt, pos, last, n_used, te, n_tiles, DISPATCH_CHUNK)
    y = _grouped_experts(xs, wg, wu, wd, tile_expert, n_used, te, n_tiles)
    return _gather_combine(x, y, rec, pos, tm, final_g=final_g)


def _chunk_experts(wg, wu, wd):
    e, d, f = wg.shape
    c = -(-f // MOE_CHUNK)
    fc = MOE_CHUNK

    def body(wg_ref, wu_ref, wd_ref, og_ref, ou_ref, od_ref):
        valid = f - pl.program_id(1) * fc
        col_ok = lax.broadcasted_iota(jnp.int32, (d, fc), 1) < valid
        row_ok = lax.broadcasted_iota(jnp.int32, (fc, d), 0) < valid
        og_ref[...] = jnp.where(col_ok, wg_ref[...], 0.0).astype(BF16)
        ou_ref[...] = jnp.where(col_ok, wu_ref[...], 0.0).astype(BF16)
        od_ref[...] = jnp.where(row_ok, wd_ref[...], 0.0).astype(BF16)

    cols_in = pl.BlockSpec((None, d, fc), lambda i, j: (i, 0, j))
    rows_in = pl.BlockSpec((None, fc, d), lambda i, j: (i, j, 0))
    cols_out = pl.BlockSpec((None, None, d, fc), lambda i, j: (i, j, 0, 0))
    rows_out = pl.BlockSpec((None, None, fc, d), lambda i, j: (i, j, 0, 0))
    return pl.pallas_call(
        body,
        grid=(e, c),
        in_specs=[cols_in, cols_in, rows_in],
        out_specs=[cols_out, cols_out, rows_out],
        out_shape=[jax.ShapeDtypeStruct((e, c, d, fc), BF16),
                   jax.ShapeDtypeStruct((e, c, d, fc), BF16),
                   jax.ShapeDtypeStruct((e, c, fc, d), BF16)],
        compiler_params=_cparams(("parallel", "parallel")),
        name="chunk_experts",
    )(wg, wu, wd)


def kernel(x_prompt, x_sample, mem_prompt, state_pool, state_conv, state_h, cache_mem_k, cache_mem_v,
           norm_mix, w_in, pool_w, pool_scale, conv_w, conv_b, gate_a_w, gate_a_b, gate_x_w, gate_x_b,
           rglru_lambda, out_norm_pool, out_norm_rnn, w_out, norm_mem, mem_norm, w_mq, w_mk, w_mv, w_mo,
           norm_ffn, ffn_w_gate, ffn_w_up, ffn_w_down, router_w, moe_w_gate, moe_w_up, moe_w_down,
           final_norm):
    P = dict(norm_mix=norm_mix, w_in=w_in, pool_w=pool_w, pool_scale=pool_scale, conv_w=conv_w,
             conv_b=conv_b, gate_a_w=gate_a_w, gate_a_b=gate_a_b, gate_x_w=gate_x_w,
             gate_x_b=gate_x_b, rglru_lambda=rglru_lambda, out_norm_pool=out_norm_pool,
             out_norm_rnn=out_norm_rnn, w_out=w_out, norm_mem=norm_mem, w_mq=w_mq, w_mo=w_mo,
             norm_ffn=norm_ffn, ffn_w_gate=ffn_w_gate, ffn_w_up=ffn_w_up, ffn_w_down=ffn_w_down,
             router_w=router_w, moe_w_gate=moe_w_gate, moe_w_up=moe_w_up, moe_w_down=moe_w_down)
    bsz, t_len, d = x_prompt.shape
    n_dec = x_sample.shape[0]
    n_tok = bsz * t_len
    fg = final_norm.reshape(1, d)
    TM, TT, TM_DEC, BB = 512, 512, n_dec, SUBLANES

    xp = x_prompt
    xs = x_sample.reshape(n_dec, d)
    mem2 = mem_prompt.reshape(bsz * N_MEM, d)
    pools_p, convs_p, hs_p, pools_s, convs_s, hs_s = ([] for _ in range(6))
    mem_k, mem_v, mem_kh, mem_vh = _mem_kv(mem2, mem_norm.reshape(DEPTH, 1, d), w_mk, w_mv, TM)
    mem_k = mem_k.reshape(DEPTH, bsz, N_MEM, d)
    mem_v = mem_v.reshape(DEPTH, bsz, N_MEM, d)
    zeros_hist = jnp.zeros((bsz, POOL_PAD, D_POOL), F32)
    zeros_conv = jnp.zeros((bsz, CONV_PAD, D_RNN), F32)
    zeros_h = jnp.zeros((bsz, 1, D_RNN), F32)

    for l in range(DEPTH):
        lw = _layer_weights(l, P)
        last_g = fg if l == DEPTH - 1 else None

        x2, h_last, pool_tail, conv_tail = _mixer_prompt(xp, zeros_hist, zeros_conv, zeros_h, lw,
                                                         0, TT)
        pools_p.append(pool_tail[:, POOL_PAD - POOL_HIST:, :])
        convs_p.append(conv_tail[:, CONV_PAD - (CONV_WIDTH - 1):, :])
        hs_p.append(h_last[:, 0, :])
        x3 = _attn_prompt(x2, mem_k, mem_v, l, lw["norm_mem"], lw["w_mq"], lw["w_mo"], TT)
        xp = _channel_mixer(x3.reshape(n_tok, d), l, lw, P, TM, last_g, True)
        xp = xp.reshape(bsz, t_len, d)

        proj_s = _norm_matmul(xs, lw["norm_mix"], lw["w_in"], TM_DEC)
        mixed_s, h_new = _mixer_decode(proj_s, jnp.swapaxes(state_pool[l], 0, 1),
                                       jnp.swapaxes(state_conv[l], 0, 1), state_h[l], lw,
                                       PAST_LEN)
        pools_s.append(jnp.concatenate([state_pool[l][:, 1:], proj_s[:, None, :D_POOL]], axis=1))
        convs_s.append(jnp.concatenate([state_conv[l][:, 1:], proj_s[:, None, D_POOL:D_MIX]],
                                       axis=1))
        hs_s.append(h_new)
        xs2 = _matmul_res(mixed_s, lw["w_out"], xs, TM_DEC)
        q_s = _norm_matmul(xs2, lw["norm_mem"], lw["w_mq"], TM_DEC)
        att = _attn_decode(q_s.reshape(n_dec, MEM_HEADS, MEM_HEAD_DIM), cache_mem_k, cache_mem_v,
                           l, BB)
        xs3 = _matmul_res(att.reshape(n_dec, d), lw["w_mo"], xs2, TM_DEC)
        xs = _channel_mixer(xs3, l, lw, P, TM_DEC, last_g, False)

    kv_shape = (DEPTH, bsz, N_MEM, MEM_HEADS, MEM_HEAD_DIM)
    return (xp, xs.reshape(n_dec, 1, d), jnp.stack(pools_p), jnp.stack(convs_p), jnp.stack(hs_p),
            mem_kh.reshape(kv_shape), mem_vh.reshape(kv_shape), jnp.stack(pools_s),
            jnp.stack(convs_s), jnp.stack(hs_s))
```

```python
import functools
import math

import jax
import jax.numpy as jnp
from jax import lax
from jax.experimental import pallas as pl
from jax.experimental.pallas import tpu as pltpu

F32 = jnp.float32
BF16 = jnp.bfloat16

DEPTH = 2
PAST_LEN = 16384
D_POOL = 512
POOL_WINDOWS = (2, 4, 8, 16)
POOL_GROUP = 128
POOL_HIST = 15
D_RNN = 512
CONV_WIDTH = 4
RGLRU_C = 8.0
D_MIX = 1024
D_IN = 1536
N_MEM = 256
MEM_HEADS = 4
MEM_HEAD_DIM = 256
N_EXPERTS = 8
EPS = 1e-6

SUBLANES = 8
LANES = 128
MXU_TILE = 256
POOL_PAD = 16
CONV_PAD = 8
VMEM_LIMIT = 56 * 1024 * 1024
MIXER_PARTS = 4
DISPATCH_CHUNK = 2048
MOE_CHUNK = MXU_TILE
MOE_TILE = 576


def _cparams(sem):
    return pltpu.CompilerParams(dimension_semantics=sem, vmem_limit_bytes=VMEM_LIMIT)


def _rms(x, g):
    ms = jnp.mean(x * x, axis=-1, keepdims=True)
    return x * lax.rsqrt(ms + EPS) * g


def _gelu_tanh(x):
    c = math.sqrt(2.0 / math.pi)
    return 0.5 * x * (1.0 + jnp.tanh(c * (x + 0.044715 * (x * x * x))))


def _softplus(x):
    return jnp.maximum(x, 0.0) + jnp.log1p(jnp.exp(-jnp.abs(x)))


def _norm_matmul_body(x_ref, g_ref, w_ref, o_ref):
    hn = _rms(x_ref[...], g_ref[...]).astype(BF16)
    o_ref[...] = jnp.dot(hn, w_ref[...], preferred_element_type=F32)


def _norm_matmul(x, g, w, tm):
    n, d = x.shape
    m = w.shape[1]
    return pl.pallas_call(
        _norm_matmul_body,
        grid=(n // tm,),
        in_specs=[pl.BlockSpec((tm, d), lambda i: (i, 0)),
                  pl.BlockSpec((1, d), lambda i: (0, 0)),
                  pl.BlockSpec((d, m), lambda i: (0, 0))],
        out_specs=pl.BlockSpec((tm, m), lambda i: (i, 0)),
        out_shape=jax.ShapeDtypeStruct((n, m), F32),
        compiler_params=_cparams(("parallel",)),
        name="norm_matmul",
    )(x, g, w)


def _mem_kv_body(x_ref, g_ref, wk_ref, wv_ref, k_ref, v_ref, kh_ref, vh_ref):
    hn = _rms(x_ref[...], g_ref[...]).astype(BF16)
    for w_ref, flat_ref, heads_ref in ((wk_ref, k_ref, kh_ref), (wv_ref, v_ref, vh_ref)):
        y = jnp.dot(hn, w_ref[...].astype(BF16), preferred_element_type=F32)
        flat_ref[...] = y
        heads_ref[...] = y.reshape(heads_ref.shape)


def _mem_kv(mem, g, wk, wv, tm):
    n, d = mem.shape
    n_layers = wk.shape[0]
    w_spec = pl.BlockSpec((None, d, d), lambda l, i: (l, 0, 0))
    flat = pl.BlockSpec((None, tm, d), lambda l, i: (l, i, 0))
    heads = pl.BlockSpec((None, tm, MEM_HEADS, MEM_HEAD_DIM), lambda l, i: (l, i, 0, 0))
    return pl.pallas_call(
        _mem_kv_body,
        grid=(n_layers, n // tm),
        in_specs=[pl.BlockSpec((tm, d), lambda l, i: (i, 0)),
                  pl.BlockSpec((None, 1, d), lambda l, i: (l, 0, 0)), w_spec, w_spec],
        out_specs=[flat, flat, heads, heads],
        out_shape=[jax.ShapeDtypeStruct((n_layers, n, d), F32)] * 2
        + [jax.ShapeDtypeStruct((n_layers, n, MEM_HEADS, MEM_HEAD_DIM), F32)] * 2,
        compiler_params=_cparams(("parallel", "parallel")),
        name="mem_kv",
    )(mem, g, wk, wv)


def _matmul_res_body(a_ref, w_ref, r_ref, o_ref):
    o_ref[...] = r_ref[...] + jnp.dot(a_ref[...].astype(BF16), w_ref[...],
                                      preferred_element_type=F32)


def _matmul_res(a, w, res, tm):
    n, k = a.shape
    m = w.shape[1]
    return pl.pallas_call(
        _matmul_res_body,
        grid=(n // tm,),
        in_specs=[pl.BlockSpec((tm, k), lambda i: (i, 0)),
                  pl.BlockSpec((k, m), lambda i: (0, 0)),
                  pl.BlockSpec((tm, m), lambda i: (i, 0))],
        out_specs=pl.BlockSpec((tm, m), lambda i: (i, 0)),
        out_shape=jax.ShapeDtypeStruct((n, m), F32),
        compiler_params=_cparams(("parallel",)),
        name="matmul_res",
    )(a, w, res)


def _pool_project(z_groups, pw_ref, scale):
    ys = [jnp.dot(z.astype(BF16), pw_ref[g], preferred_element_type=F32)
          for g, z in enumerate(z_groups)]
    return jnp.concatenate(ys, axis=-1) * scale


def _rglru_terms(c, first_pos, wa_ref, ba_ref, wx_ref, bx_ref, lam_ref):
    cb = c.astype(BF16)
    r = jax.nn.sigmoid(jnp.dot(cb, wa_ref[...], preferred_element_type=F32) + ba_ref[...])
    i = jax.nn.sigmoid(jnp.dot(cb, wx_ref[...], preferred_element_type=F32) + bx_ref[...])
    log_a = (-RGLRU_C) * r * _softplus(-lam_ref[...])
    a = jnp.exp(log_a)
    mult = jnp.sqrt(1.0 - a * a)
    if first_pos is not None:
        mult = jnp.where(first_pos, 1.0, mult)
    return a, mult * (i * c)


def _mix_out(y_pool, h, u_gate, gp_ref, gr_ref):
    y_rnn = h * _gelu_tanh(u_gate)
    return jnp.concatenate([_rms(y_pool, gp_ref[...]), _rms(y_rnn, gr_ref[...])], axis=-1)


def _mixer_body(pos0, tt, x_ref, gm_ref, win_ref, wout_ref, hist_ref, conv_ref, h0_ref,
                pw_ref, ps_ref, cw_ref, cb_ref, wa_ref, ba_ref, wx_ref, bx_ref, lam_ref, gp_ref,
                gr_ref, o_ref, hl_ref, ptail_ref, ctail_ref, proj_ref, pbuf, cbuf, a_s, b_s, yp_s,
                hc):
    t = pl.program_id(1)

    @pl.when(t == 0)
    def _():
        pbuf[0:POOL_PAD, :] = hist_ref[...]
        cbuf[0:CONV_PAD, :] = conv_ref[...]
        hc[...] = jnp.broadcast_to(h0_ref[...], (SUBLANES, D_RNN))

    @pl.when(t > 0)
    def _():
        pbuf[0:POOL_PAD, :] = pbuf[tt:tt + POOL_PAD, :]
        cbuf[0:CONV_PAD, :] = cbuf[tt:tt + CONV_PAD, :]

    part = tt // MIXER_PARTS
    for r0 in range(0, tt, part):
        rows = slice(r0, r0 + part)
        hn = _rms(x_ref[rows, :], gm_ref[...]).astype(BF16)
        proj_ref[rows, :] = jnp.dot(hn, win_ref[...], preferred_element_type=F32)
        u_pool = proj_ref[rows, 0:D_POOL]
        pbuf[POOL_PAD + r0:POOL_PAD + r0 + part, :] = u_pool
        cbuf[CONV_PAD + r0:CONV_PAD + r0 + part, :] = proj_ref[rows, D_POOL:D_MIX]

        pos = pos0 + t * tt + r0 + lax.broadcasted_iota(jnp.int32, (part, 1), 0)

        zs = []
        for g, w in enumerate(POOL_WINDOWS):
            sl = slice(g * POOL_GROUP, (g + 1) * POOL_GROUP)
            base = POOL_PAD + r0
            acc = pbuf[base:base + part, sl]
            for k in range(1, w):
                acc = acc + pbuf[base - k:base - k + part, sl]
            cnt = jnp.minimum(pos + 1, w).astype(F32)
            zs.append(acc / cnt - u_pool[:, sl])
        yp_s[rows, :] = _pool_project(zs, pw_ref, ps_ref[...])

        c = cb_ref[...]
        for k in range(CONV_WIDTH):
            off = CONV_PAD + r0 - (CONV_WIDTH - 1) + k
            c = c + cbuf[off:off + part, :] * cw_ref[k:k + 1, :]

        a, b = _rglru_terms(c, pos == 0, wa_ref, ba_ref, wx_ref, bx_ref, lam_ref)
        a_s[rows, :] = a
        b_s[rows, :] = b

    row = lax.broadcasted_iota(jnp.int32, (SUBLANES, D_RNN), 0)

    def group(j, carry):
        r0 = pl.multiple_of(j * SUBLANES, SUBLANES)
        ga = a_s[pl.ds(r0, SUBLANES), :]
        gb = b_s[pl.ds(r0, SUBLANES), :]
        for s in (1, 2, 4):
            keep = row >= s
            gb = jnp.where(keep, ga * pltpu.roll(gb, s, axis=0) + gb, gb)
            ga = jnp.where(keep, ga * pltpu.roll(ga, s, axis=0), ga)
        h = ga * carry + gb
        b_s[pl.ds(r0, SUBLANES), :] = h
        return jnp.broadcast_to(h[SUBLANES - 1:SUBLANES, :], (SUBLANES, D_RNN))

    carry = lax.fori_loop(0, tt // SUBLANES, group, hc[...], unroll=4)
    hc[...] = carry
    hl_ref[...] = carry

    for r0 in range(0, tt, part):
        rows = slice(r0, r0 + part)
        mix = _mix_out(yp_s[rows, :], b_s[rows, :], proj_ref[rows, D_MIX:D_IN], gp_ref, gr_ref)
        o_ref[rows, :] = x_ref[rows, :] + jnp.dot(mix.astype(BF16), wout_ref[...],
                                                  preferred_element_type=F32)
    ptail_ref[...] = pbuf[tt:tt + POOL_PAD, :]
    ctail_ref[...] = cbuf[tt:tt + CONV_PAD, :]


def _mixer_prompt(x, hist, conv, h0, lw, pos0, tt):
    bsz, t_len, d = x.shape
    full = lambda shape: pl.BlockSpec(shape, lambda b, t: (0,) * len(shape))
    per_b = lambda rows, cols: pl.BlockSpec((None, rows, cols), lambda b, t: (b, 0, 0))
    xs = pl.BlockSpec((None, tt, d), lambda b, t: (b, t, 0))
    return pl.pallas_call(
        functools.partial(_mixer_body, pos0, tt),
        grid=(bsz, t_len // tt),
        in_specs=[xs, full((1, d)), full((d, D_IN)), full((D_MIX, d)),
                  per_b(POOL_PAD, D_POOL), per_b(CONV_PAD, D_RNN), per_b(1, D_RNN),
                  full((len(POOL_WINDOWS), POOL_GROUP, POOL_GROUP)), full((1, D_POOL)),
                  full((CONV_WIDTH, D_RNN)), full((1, D_RNN)),
                  full((D_RNN, D_RNN)), full((1, D_RNN)),
                  full((D_RNN, D_RNN)), full((1, D_RNN)),
                  full((1, D_RNN)), full((1, D_POOL)), full((1, D_RNN))],
        out_specs=[xs, per_b(SUBLANES, D_RNN), per_b(POOL_PAD, D_POOL), per_b(CONV_PAD, D_RNN)],
        out_shape=[jax.ShapeDtypeStruct((bsz, t_len, d), F32),
                   jax.ShapeDtypeStruct((bsz, SUBLANES, D_RNN), F32),
                   jax.ShapeDtypeStruct((bsz, POOL_PAD, D_POOL), F32),
                   jax.ShapeDtypeStruct((bsz, CONV_PAD, D_RNN), F32)],
        scratch_shapes=[pltpu.VMEM((tt, D_IN), F32),
                        pltpu.VMEM((POOL_PAD + tt, D_POOL), F32),
                        pltpu.VMEM((CONV_PAD + tt, D_RNN), F32),
                        pltpu.VMEM((tt, D_RNN), F32),
                        pltpu.VMEM((tt, D_RNN), F32),
                        pltpu.VMEM((tt, D_POOL), F32),
                        pltpu.VMEM((SUBLANES, D_RNN), F32)],
        compiler_params=_cparams(("parallel", "arbitrary")),
        name="mixer_prompt",
    )(x, lw["norm_mix"], lw["w_in"], lw["w_out"], hist, conv, h0,
      lw["pool_w"], lw["pool_scale"], lw["conv_w"], lw["conv_b"],
      lw["gate_a_w"], lw["gate_a_b"], lw["gate_x_w"], lw["gate_x_b"], lw["lam"],
      lw["out_norm_pool"], lw["out_norm_rnn"])


def _mixer_decode_body(pos0, proj_ref, hist_ref, conv_ref, h0_ref, pw_ref, ps_ref, cw_ref, cb_ref,
                       wa_ref, ba_ref, wx_ref, bx_ref, lam_ref, gp_ref, gr_ref, mix_ref, hn_ref):
    u_pool = proj_ref[:, 0:D_POOL]
    zs = []
    for g, w in enumerate(POOL_WINDOWS):
        sl = slice(g * POOL_GROUP, (g + 1) * POOL_GROUP)
        acc = u_pool[:, sl]
        for k in range(1, w):
            acc = acc + hist_ref[POOL_HIST - k, :, sl]
        zs.append(acc / float(min(pos0 + 1, w)) - u_pool[:, sl])
    y_pool = _pool_project(zs, pw_ref, ps_ref[...])

    c = cb_ref[...]
    for k in range(CONV_WIDTH - 1):
        c = c + conv_ref[k] * cw_ref[k:k + 1, :]
    c = c + proj_ref[:, D_POOL:D_MIX] * cw_ref[CONV_WIDTH - 1:CONV_WIDTH, :]

    a, b = _rglru_terms(c, None, wa_ref, ba_ref, wx_ref, bx_ref, lam_ref)
    h = a * h0_ref[...] + b
    hn_ref[...] = h
    mix_ref[...] = _mix_out(y_pool, h, proj_ref[:, D_MIX:D_IN], gp_ref, gr_ref).astype(BF16)


def _mixer_decode(proj, hist_t, conv_t, h0, lw, pos0):
    assert pos0 > 0
    n = proj.shape[0]
    vm = pl.BlockSpec(memory_space=pltpu.VMEM)
    return pl.pallas_call(
        functools.partial(_mixer_decode_body, pos0),
        in_specs=[vm] * 15,
        out_specs=[vm, vm],
        out_shape=[jax.ShapeDtypeStruct((n, D_MIX), BF16),
                   jax.ShapeDtypeStruct((n, D_RNN), F32)],
        compiler_params=pltpu.CompilerParams(vmem_limit_bytes=VMEM_LIMIT),
        name="mixer_decode",
    )(proj, hist_t, conv_t, h0, lw["pool_w"], lw["pool_scale"], lw["conv_w"], lw["conv_b"],
      lw["gate_a_w"], lw["gate_a_b"], lw["gate_x_w"], lw["gate_x_b"], lw["lam"],
      lw["out_norm_pool"], lw["out_norm_rnn"])


def _softmax_rows(s):
    e = jnp.exp(s - jnp.max(s, axis=-1, keepdims=True))
    return e / jnp.sum(e, axis=-1, keepdims=True)


_NT = (((1,), (1,)), ((), ()))


def _attn_prompt_body(x_ref, k_ref, v_ref, g_ref, wq_ref, wo_ref, o_ref):
    x = x_ref[...]
    xn = _rms(x, g_ref[...]).astype(BF16)
    q = jnp.dot(xn, wq_ref[...], preferred_element_type=F32)
    outs = []
    for h in range(MEM_HEADS):
        sl = slice(h * MEM_HEAD_DIM, (h + 1) * MEM_HEAD_DIM)
        s = lax.dot_general(q[:, sl].astype(BF16), k_ref[:, sl].astype(BF16), _NT,
                            preferred_element_type=F32) * (MEM_HEAD_DIM ** -0.5)
        p = _softmax_rows(s).astype(BF16)
        outs.append(jnp.dot(p, v_ref[:, sl].astype(BF16), preferred_element_type=F32))
    o = jnp.concatenate(outs, axis=-1).astype(BF16)
    o_ref[...] = x + jnp.dot(o, wo_ref[...], preferred_element_type=F32)


def _attn_prompt(x, k, v, layer, g, wq, wo, tt):
    bsz, t_len, d = x.shape
    full = lambda shape: pl.BlockSpec(shape, lambda b, t: (0,) * len(shape))
    kv = pl.BlockSpec((None, None, N_MEM, d), lambda b, t: (layer, b, 0, 0))
    xs = pl.BlockSpec((None, tt, d), lambda b, t: (b, t, 0))
    return pl.pallas_call(
        _attn_prompt_body,
        grid=(bsz, t_len // tt),
        in_specs=[xs, kv, kv, full((1, d)), full((d, d)), full((d, d))],
        out_specs=xs,
        out_shape=jax.ShapeDtypeStruct(x.shape, F32),
        compiler_params=_cparams(("parallel", "parallel")),
        name="attn_prompt",
    )(x, k, v, g, wq, wo)


def _attn_decode_body(bb, q_ref, k_ref, v_ref, o_ref):
    for b in range(bb):
        k = k_ref[b]
        q = q_ref[b] * (MEM_HEAD_DIM ** -0.5)
        s = jnp.broadcast_to(jnp.sum(k * q[None], axis=-1, keepdims=True), k.shape)
        e = jnp.exp(s - jnp.max(s, axis=0, keepdims=True))
        o_ref[b] = jnp.sum(e * v_ref[b], axis=0) / jnp.sum(e, axis=0)


def _attn_decode_mxu_body(layer, bb, q_ref, k_hbm, v_hbm, o_ref, kbuf, vbuf, sem):
    i = pl.program_id(0)
    cur = i % 2

    def copies(step, buf):
        rows = pl.ds(step * bb, bb)
        cps = []
        for h in range(MEM_HEADS):
            cps.append(pltpu.make_async_copy(k_hbm.at[layer, rows, :, h, :], kbuf.at[buf, h],
                                             sem.at[buf]))
            cps.append(pltpu.make_async_copy(v_hbm.at[layer, rows, :, h, :], vbuf.at[buf, h],
                                             sem.at[buf]))
        return cps

    @pl.when(i == 0)
    def _():
        for cp in copies(0, 0):
            cp.start()

    @pl.when(i + 1 < pl.num_programs(0))
    def _():
        for cp in copies(i + 1, 1 - cur):
            cp.start()

    for cp in copies(i, cur):
        cp.wait()

    rep = 2 * SUBLANES
    pairs = [(b, h) for b in range(bb) for h in range(MEM_HEADS)]
    scores = []
    for b, h in pairs:
        qh = jnp.broadcast_to(q_ref[b, h:h + 1, :] * (MEM_HEAD_DIM ** -0.5),
                              (rep, MEM_HEAD_DIM)).astype(BF16)
        scores.append(lax.dot_general(qh, kbuf[cur, h, b].astype(BF16), _NT,
                                      preferred_element_type=F32))
    p_all = _softmax_rows(jnp.concatenate(scores, axis=0)).astype(BF16)
    for j, (b, h) in enumerate(pairs):
        o = jnp.dot(p_all[j * rep:(j + 1) * rep, :], vbuf[cur, h, b].astype(BF16),
                    preferred_element_type=F32)
        o_ref[b, h:h + 1, :] = o[0:1, :]


def _attn_decode_mxu(q, k, v, layer, bb):
    n, nh, hd = q.shape
    qs = pl.BlockSpec((bb, nh, hd), lambda i: (i, 0, 0))
    any_spec = pl.BlockSpec(memory_space=pl.ANY)
    return pl.pallas_call(
        functools.partial(_attn_decode_mxu_body, layer, bb),
        grid=(n // bb,),
        in_specs=[qs, any_spec, any_spec],
        out_specs=qs,
        out_shape=jax.ShapeDtypeStruct((n, nh, hd), F32),
        scratch_shapes=[pltpu.VMEM((2, nh, bb, N_MEM, hd), F32),
                        pltpu.VMEM((2, nh, bb, N_MEM, hd), F32),
                        pltpu.SemaphoreType.DMA((2,))],
        compiler_params=_cparams(("arbitrary",)),
        name="attn_decode_mxu",
    )(q, k, v)


def _attn_decode(q, k, v, layer, bb):
    n, nh, hd = q.shape
    kv = pl.BlockSpec((None, bb, N_MEM, nh, hd), lambda i: (layer, i, 0, 0, 0))
    qs = pl.BlockSpec((bb, nh, hd), lambda i: (i, 0, 0))
    return pl.pallas_call(
        functools.partial(_attn_decode_body, bb),
        grid=(n // bb,),
        in_specs=[qs, kv, kv],
        out_specs=qs,
        out_shape=jax.ShapeDtypeStruct((n, nh, hd), F32),
        compiler_params=_cparams(("parallel",)),
        name="attn_decode",
    )(q, k, v)


def _top2(hn, rw_ref):
    rw = rw_ref[...]
    hn_hi, rw_hi = hn.astype(BF16), rw.astype(BF16)
    hn_lo = (hn - hn_hi.astype(F32)).astype(BF16)
    rw_lo = (rw - rw_hi.astype(F32)).astype(BF16)
    logits = (jnp.dot(hn_hi, rw_hi, preferred_element_type=F32)
              + jnp.dot(hn_lo, rw_hi, preferred_element_type=F32)
              + jnp.dot(hn_hi, rw_lo, preferred_element_type=F32))
    lane = lax.broadcasted_iota(jnp.int32, logits.shape, 1)
    neg = -jnp.inf
    lg = jnp.where(lane < N_EXPERTS, logits, neg)
    m1 = jnp.max(lg, axis=-1, keepdims=True)
    i1 = jnp.min(jnp.where(lg == m1, lane, LANES), axis=-1, keepdims=True)
    lg2 = jnp.where(lane == i1, neg, lg)
    m2 = jnp.max(lg2, axis=-1, keepdims=True)
    i2 = jnp.min(jnp.where(lg2 == m2, lane, LANES), axis=-1, keepdims=True)
    e2 = jnp.exp(m2 - m1)
    den = 1.0 + e2
    return lane, i1, i2, 1.0 / den, e2 / den


def _router_body(x_ref, g_ref, rw_ref, cw_ref):
    lane, i1, i2, g1, g2 = _top2(_rms(x_ref[...], g_ref[...]), rw_ref)
    cw_ref[...] = jnp.where(lane == i1, g1, 0.0) + jnp.where(lane == i2, g2, 0.0)


def _router(x, g, rw, tm):
    n, d = x.shape
    return pl.pallas_call(
        _router_body,
        grid=(n // tm,),
        in_specs=[pl.BlockSpec((tm, d), lambda i: (i, 0)),
                  pl.BlockSpec((1, d), lambda i: (0, 0)),
                  pl.BlockSpec((d, LANES), lambda i: (0, 0))],
        out_specs=pl.BlockSpec((tm, LANES), lambda i: (i, 0)),
        out_shape=jax.ShapeDtypeStruct((n, LANES), F32),
        compiler_params=_cparams(("parallel",)),
        name="router",
    )(x, g, rw)


def _store_row_tiles(ref, val):
    rows = val.shape[0]
    for c in range(val.shape[1] // LANES):
        ref[pl.ds(c, rows, stride=SUBLANES), :] = val[:, c * LANES:(c + 1) * LANES]


def _load_row_tiles(ref, rows):
    return jnp.concatenate([ref[pl.ds(c, rows, stride=SUBLANES), :] for c in range(SUBLANES)],
                           axis=-1)


R_E1, R_E2, R_RANK1, R_RANK2, R_G1, R_G2 = range(6)


def _router_rank_body(x_ref, g_ref, rw_ref, rec_ref, rows_ref, cnt_ref, hn_ref, carry):
    i = pl.program_id(0)
    tm = x_ref.shape[0]

    @pl.when(i == 0)
    def _():
        carry[...] = jnp.zeros_like(carry)

    hn = _rms(x_ref[...], g_ref[...])
    _store_row_tiles(hn_ref, hn)
    lane, i1, i2, g1, g2 = _top2(hn, rw_ref)
    picks = ((lane == i1) | (lane == i2)).astype(F32)
    r_io = lax.broadcasted_iota(jnp.int32, (tm, tm), 0)
    c_io = lax.broadcasted_iota(jnp.int32, (tm, tm), 1)
    earlier = (c_io < r_io).astype(BF16)
    before = jnp.dot(earlier, picks.astype(BF16), preferred_element_type=F32) + carry[...]
    rank1 = jnp.sum(jnp.where(lane == i1, before, 0.0), axis=-1, keepdims=True)
    rank2 = jnp.sum(jnp.where(lane == i2, before, 0.0), axis=-1, keepdims=True)
    carry[...] += jnp.sum(picks, axis=0, keepdims=True)
    cnt_ref[...] = jnp.broadcast_to(carry[...], cnt_ref.shape)

    rec = jnp.zeros((tm, LANES), F32)
    for k, v in ((R_E1, i1.astype(F32)), (R_E2, i2.astype(F32)), (R_RANK1, rank1),
                 (R_RANK2, rank2), (R_G1, g1), (R_G2, g2)):
        rec = jnp.where(lane == k, v, rec)
    rec_ref[...] = rec
    rows_ref[...] = rec.T[0:SUBLANES, :].astype(jnp.int32)


def _router_rank(x, g, rw, tm):
    n, d = x.shape
    assert d == SUBLANES * LANES
    return pl.pallas_call(
        _router_rank_body,
        grid=(n // tm,),
        in_specs=[pl.BlockSpec((tm, d), lambda i: (i, 0)),
                  pl.BlockSpec((1, d), lambda i: (0, 0)),
                  pl.BlockSpec((d, LANES), lambda i: (0, 0))],
        out_specs=[pl.BlockSpec((tm, LANES), lambda i: (i, 0)),
                   pl.BlockSpec((SUBLANES, tm), lambda i: (0, i)),
                   pl.BlockSpec((SUBLANES, LANES), lambda i: (0, 0)),
                   pl.BlockSpec((tm * SUBLANES, LANES), lambda i: (i, 0))],
        out_shape=[jax.ShapeDtypeStruct((n, LANES), F32),
                   jax.ShapeDtypeStruct((SUBLANES, n), jnp.int32),
                   jax.ShapeDtypeStruct((SUBLANES, LANES), F32),
                   jax.ShapeDtypeStruct((n * SUBLANES, LANES), F32)],
        scratch_shapes=[pltpu.VMEM((1, LANES), F32)],
        compiler_params=_cparams(("arbitrary",)),
        name="router_rank",
    )(x, g, rw)


def _tile_rows(r):
    return pl.ds(pl.multiple_of(r * SUBLANES, SUBLANES), SUBLANES)


def _route_plan_body(tm, n_tiles, cnt_ref, rows_ref, pos_ref, te_ref, nu_ref, last_ref, off_s):
    c = pl.program_id(0)

    @pl.when(c == 0)
    def _():
        def per_expert(e, tile0):
            n = cnt_ref[e]
            off_s[e] = tile0 * tm
            tile1 = tile0 + lax.div(n + (tm - 1), tm)

            def fill_tile(t, z):
                te_ref[t] = e
                return z

            lax.fori_loop(tile0, tile1, fill_tile, 0)
            last_ref[e] = jnp.where(tile1 > tile0, tile1 - 1, -1)
            return tile1

        used = lax.fori_loop(0, N_EXPERTS, per_expert, jnp.int32(0))
        nu_ref[0] = used

        def fill_rest(t, z):
            te_ref[t] = N_EXPERTS - 1
            return z

        lax.fori_loop(used, n_tiles, fill_rest, 0)

    rows = rows_ref[...]
    pos = rows[R_RANK1:R_RANK2 + 1, :]
    for e in range(N_EXPERTS):
        pos = pos + jnp.where(rows[R_E1:R_E2 + 1, :] == e, off_s[e], 0)
    pos_ref[...] = jnp.zeros_like(pos_ref)
    pos_ref[0:2, :] = pos


def _route_plan(counts, rows, tm, n_tiles, tc):
    n_tok = rows.shape[1]
    smem = pl.BlockSpec(memory_space=pltpu.SMEM)
    blk = pl.BlockSpec((SUBLANES, tc), lambda c: (0, c))
    return pl.pallas_call(
        functools.partial(_route_plan_body, tm, n_tiles),
        grid=(n_tok // tc,),
        in_specs=[smem, blk],
        out_specs=[blk, smem, smem, smem],
        out_shape=[jax.ShapeDtypeStruct((SUBLANES, n_tok), jnp.int32),
                   jax.ShapeDtypeStruct((n_tiles,), jnp.int32),
                   jax.ShapeDtypeStruct((1,), jnp.int32),
                   jax.ShapeDtypeStruct((N_EXPERTS,), jnp.int32)],
        scratch_shapes=[pltpu.SMEM((N_EXPERTS,), jnp.int32)],
        compiler_params=_cparams(("arbitrary",)),
        name="route_plan",
    )(counts, rows)


def _dispatch_body(tm, n_tiles, last_ref, nu_ref, pos_ref, xt_ref, xs_hbm, zbuf, sem, zsem):
    c = pl.program_id(0)
    tc = pos_ref.shape[1]

    def zero_tile(t):
        rows = pl.ds(pl.multiple_of(t * (tm * SUBLANES), SUBLANES), tm * SUBLANES)
        cp = pltpu.make_async_copy(zbuf, xs_hbm.at[rows, :], zsem.at[0])
        cp.start()
        cp.wait()

    @pl.when(c == 0)
    def _():
        zbuf[...] = jnp.zeros_like(zbuf)
        for e in range(N_EXPERTS):
            @pl.when(last_ref[e] >= 0)
            def _():
                zero_tile(last_ref[e])

        def zero_unused(t, z):
            zero_tile(t)
            return z

        lax.fori_loop(nu_ref[0], n_tiles, zero_unused, 0)

    def row_copy(j, p):
        return pltpu.make_async_copy(xt_ref.at[_tile_rows(j), :], xs_hbm.at[_tile_rows(p), :],
                                     sem.at[0])

    def issue(j, z):
        row_copy(j, pos_ref[0, j]).start()
        row_copy(j, pos_ref[1, j]).start()
        return z

    lax.fori_loop(0, tc, issue, 0, unroll=8)

    def drain(j, z):
        row_copy(0, 0).wait()
        row_copy(0, 0).wait()
        return z

    lax.fori_loop(0, tc, drain, 0, unroll=8)


def _dispatch(xt, pos, last, n_used, tm, n_tiles, tc):
    n_tok = xt.shape[0] // SUBLANES
    smem = pl.BlockSpec(memory_space=pltpu.SMEM)
    return pl.pallas_call(
        functools.partial(_dispatch_body, tm, n_tiles),
        grid=(n_tok // tc,),
        in_specs=[smem, smem,
                  pl.BlockSpec((SUBLANES, tc), lambda c: (0, c), memory_space=pltpu.SMEM),
                  pl.BlockSpec((tc * SUBLANES, LANES), lambda c: (c, 0))],
        out_specs=pl.BlockSpec(memory_space=pl.ANY),
        out_shape=jax.ShapeDtypeStruct((n_tiles * tm * SUBLANES, LANES), F32),
        scratch_shapes=[pltpu.VMEM((tm * SUBLANES, LANES), F32),
                        pltpu.SemaphoreType.DMA((1,)), pltpu.SemaphoreType.DMA((1,))],
        compiler_params=_cparams(("arbitrary",)),
        name="moe_dispatch",
    )(last, n_used, pos, xt)


def _grouped_experts_body(tm, te_ref, nu_ref, x_ref, wg_ref, wu_ref, wd_ref, y_ref):
    del te_ref
    i = pl.program_id(0)

    @pl.when(i < nu_ref[0])
    def _():
        hn = _load_row_tiles(x_ref, tm).astype(BF16)
        acc = None
        for k in range(wg_ref.shape[0]):
            gate = jnp.dot(hn, wg_ref[k], preferred_element_type=F32)
            up = jnp.dot(hn, wu_ref[k], preferred_element_type=F32)
            mid = (gate * jax.nn.sigmoid(gate) * up).astype(BF16)
            y = jnp.dot(mid, wd_ref[k], preferred_element_type=F32)
            acc = y if acc is None else acc + y
        _store_row_tiles(y_ref, acc)

    @pl.when(i >= nu_ref[0])
    def _():
        y_ref[...] = jnp.zeros_like(y_ref)


def _grouped_experts(xs, wg, wu, wd, tile_expert, n_used, tm, n_tiles):
    _, n_chunks, d, fc = wg.shape
    w_idx = lambda i, te, nu: (te[i], 0, 0, 0)
    grid_spec = pltpu.PrefetchScalarGridSpec(
        num_scalar_prefetch=2,
        grid=(n_tiles,),
        in_specs=[pl.BlockSpec((tm * SUBLANES, LANES),
                               lambda i, te, nu: (jnp.minimum(i, nu[0] - 1), 0)),
                  pl.BlockSpec((None, n_chunks, d, fc), w_idx),
                  pl.BlockSpec((None, n_chunks, d, fc), w_idx),
                  pl.BlockSpec((None, n_chunks, fc, d), w_idx)],
        out_specs=pl.BlockSpec((tm * SUBLANES, LANES), lambda i, te, nu: (i, 0)))
    return pl.pallas_call(
        functools.partial(_grouped_experts_body, tm),
        grid_spec=grid_spec,
        out_shape=jax.ShapeDtypeStruct(xs.shape, F32),
        compiler_params=_cparams(("arbitrary",)),
        name="grouped_experts",
    )(tile_expert, n_used, xs, wg, wu, wd)


def _gather_combine_body(final, pos_ref, posn_ref, x_ref, rec_ref, y_hbm, *rest):
    o_ref, y1buf, y2buf, sem = rest[-4:]
    i = pl.program_id(0)
    tm = x_ref.shape[0]
    cur = i % 2

    def row_copy(p, r, dst, buf):
        return pltpu.make_async_copy(y_hbm.at[_tile_rows(p), :], dst.at[buf, _tile_rows(r), :],
                                     sem.at[buf])

    def issue(pref, buf):
        def one(j, z):
            row_copy(pref[0, j], j, y1buf, buf).start()
            row_copy(pref[1, j], j, y2buf, buf).start()
            return z

        lax.fori_loop(0, tm, one, 0, unroll=8)

    @pl.when(i == 0)
    def _():
        issue(pos_ref, 0)

    @pl.when(i + 1 < pl.num_programs(0))
    def _():
        issue(posn_ref, 1 - cur)

    def drain(j, z):
        row_copy(0, 0, y1buf, cur).wait()
        row_copy(0, 0, y2buf, cur).wait()
        return z

    lax.fori_loop(0, tm, drain, 0, unroll=8)
    rec = rec_ref[...]
    out = (x_ref[...] + rec[:, R_G1:R_G1 + 1] * _load_row_tiles(y1buf.at[cur], tm)
           + rec[:, R_G2:R_G2 + 1] * _load_row_tiles(y2buf.at[cur], tm))
    o_ref[...] = _rms(out, rest[0][...]) if final else out


def _gather_combine(x, y, rec, pos, tm, final_g=None):
    n, d = x.shape
    final = final_g is not None
    nb = n // tm
    row = pl.BlockSpec((tm, d), lambda i: (i, 0))
    in_specs = [pl.BlockSpec((SUBLANES, tm), lambda i: (0, i), memory_space=pltpu.SMEM),
                pl.BlockSpec((SUBLANES, tm), lambda i: (0, jnp.minimum(i + 1, nb - 1)),
                             memory_space=pltpu.SMEM),
                row, pl.BlockSpec((tm, LANES), lambda i: (i, 0)),
                pl.BlockSpec(memory_space=pl.ANY)]
    args = [pos, pos, x, rec, y]
    if final:
        in_specs.append(pl.BlockSpec((1, d), lambda i: (0, 0)))
        args.append(final_g)
    return pl.pallas_call(
        functools.partial(_gather_combine_body, final),
        grid=(nb,),
        in_specs=in_specs,
        out_specs=row,
        out_shape=jax.ShapeDtypeStruct((n, d), F32),
        scratch_shapes=[pltpu.VMEM((2, tm * SUBLANES, LANES), F32),
                        pltpu.VMEM((2, tm * SUBLANES, LANES), F32),
                        pltpu.SemaphoreType.DMA((2,))],
        compiler_params=_cparams(("arbitrary",)),
        name="moe_gather_combine",
    )(*args)


def _ffn_dense_body(bounds, final, x_ref, g_ref, wg_ref, wu_ref, wd_ref, *rest):
    o_ref = rest[-1]
    x = x_ref[...]
    hn = _rms(x, g_ref[...]).astype(BF16)
    out = x
    for a, b in bounds:
        gate = jnp.dot(hn, wg_ref[:, a:b], preferred_element_type=F32)
        up = jnp.dot(hn, wu_ref[:, a:b], preferred_element_type=F32)
        mid = (gate * jax.nn.sigmoid(gate) * up).astype(BF16)
        out = out + jnp.dot(mid, wd_ref[a:b, :], preferred_element_type=F32)
    o_ref[...] = _rms(out, rest[0][...]) if final else out


def _ffn_dense(x, g, wg, wu, wd, tm, final_g=None):
    n, d = x.shape
    f = wg.shape[1]
    final = final_g is not None
    half = -(-f // (2 * MXU_TILE)) * MXU_TILE
    bounds = ((0, half), (half, f))
    row = pl.BlockSpec((tm, d), lambda i: (i, 0))
    vec = pl.BlockSpec((1, d), lambda i: (0, 0))
    once = lambda shape: pl.BlockSpec(shape, lambda i: (0, 0), pipeline_mode=pl.Buffered(1))
    in_specs = [row, vec, once((d, f)), once((d, f)), once((f, d))]
    args = [x, g, wg, wu, wd]
    if final:
        in_specs.append(vec)
        args.append(final_g)
    return pl.pallas_call(
        functools.partial(_ffn_dense_body, bounds, final),
        grid=(n // tm,),
        in_specs=in_specs,
        out_specs=row,
        out_shape=jax.ShapeDtypeStruct((n, d), F32),
        compiler_params=_cparams(("parallel",)),
        name="ffn_dense",
    )(*args)


def _ffn_all_experts_body(n_experts, per_expert, final, x_ref, g_ref, wg_ref, wu_ref, wd_ref,
                          cw_ref, *rest):
    o_ref, hn_s = rest[-2:]
    j = pl.program_id(1)

    @pl.when(j == 0)
    def _():
        hn_s[...] = _rms(x_ref[...], g_ref[...]).astype(BF16)
        o_ref[...] = x_ref[...]

    hn = hn_s[...]
    y = None
    for c in range(per_expert):
        gate = jnp.dot(hn, wg_ref[c], preferred_element_type=F32)
        up = jnp.dot(hn, wu_ref[c], preferred_element_type=F32)
        mid = (gate * jax.nn.sigmoid(gate) * up).astype(BF16)
        part = jnp.dot(mid, wd_ref[c], preferred_element_type=F32)
        y = part if y is None else y + part
    cw = cw_ref[...]
    lane = lax.broadcasted_iota(jnp.int32, cw.shape, 1)
    o_ref[...] += y * jnp.sum(jnp.where(lane == j, cw, 0.0), axis=-1, keepdims=True)

    if final:
        @pl.when(j == n_experts - 1)
        def _():
            o_ref[...] = _rms(o_ref[...], rest[0][...])


def _ffn_all_experts(x, g, wg, wu, wd, cw, tm, final_g=None):
    n, d = x.shape
    final = final_g is not None
    n_experts, per_expert, _, fc = wg.shape
    row = pl.BlockSpec((tm, d), lambda i, j: (i, 0))
    vec = pl.BlockSpec((1, d), lambda i, j: (0, 0))
    w_idx = lambda i, j: (j, 0, 0, 0)
    w_in_spec = pl.BlockSpec((None, per_expert, d, fc), w_idx)
    in_specs = [row, vec, w_in_spec, w_in_spec, pl.BlockSpec((None, per_expert, fc, d), w_idx),
                pl.BlockSpec((tm, LANES), lambda i, j: (i, 0))]
    args = [x, g, wg, wu, wd, cw]
    if final:
        in_specs.append(vec)
        args.append(final_g)
    return pl.pallas_call(
        functools.partial(_ffn_all_experts_body, n_experts, per_expert, final),
        grid=(n // tm, n_experts),
        in_specs=in_specs,
        out_specs=row,
        out_shape=jax.ShapeDtypeStruct((n, d), F32),
        scratch_shapes=[pltpu.VMEM((tm, d), BF16)],
        compiler_params=_cparams(("parallel", "arbitrary")),
        name="ffn_all_experts",
    )(*args)


def _block_diag(w):
    h, a, b = w.shape
    eye = jnp.eye(h, dtype=w.dtype)
    return (eye[:, None, :, None] * w[:, :, None, :]).reshape(h * a, h * b)


def _layer_weights(l, P):
    row = lambda v: v[l].reshape(1, -1)
    return dict(
        norm_mix=row(P["norm_mix"]), w_in=P["w_in"][l].astype(BF16),
        pool_w=P["pool_w"][l].astype(BF16), pool_scale=row(P["pool_scale"]),
        conv_w=P["conv_w"][l], conv_b=row(P["conv_b"]),
        gate_a_w=_block_diag(P["gate_a_w"][l]).astype(BF16), gate_a_b=row(P["gate_a_b"]),
        gate_x_w=_block_diag(P["gate_x_w"][l]).astype(BF16), gate_x_b=row(P["gate_x_b"]),
        lam=row(P["rglru_lambda"]),
        out_norm_pool=row(P["out_norm_pool"]), out_norm_rnn=row(P["out_norm_rnn"]),
        w_out=P["w_out"][l].astype(BF16), norm_mem=row(P["norm_mem"]),
        w_mq=P["w_mq"][l].astype(BF16), w_mo=P["w_mo"][l].astype(BF16),
        norm_ffn=row(P["norm_ffn"]),
        moe=(_chunk_experts(P["moe_w_gate"][l // 2], P["moe_w_up"][l // 2], P["moe_w_down"][l // 2])
             if l % 2 == 1 else None))


def _channel_mixer(x, l, lw, P, tm, final_g, sparse):
    j = l // 2
    n = x.shape[0]
    if l % 2 == 0:
        return _ffn_dense(x, lw["norm_ffn"], P["ffn_w_gate"][j].astype(BF16),
                          P["ffn_w_up"][j].astype(BF16), P["ffn_w_down"][j].astype(BF16),
                          tm, final_g=final_g)
    rw = jnp.pad(P["router_w"][j], ((0, 0), (0, LANES - N_EXPERTS)))
    wg, wu, wd = lw["moe"]
    if not sparse:
        cw = _router(x, lw["norm_ffn"], rw, tm)
        return _ffn_all_experts(x, lw["norm_ffn"], wg, wu, wd, cw, tm, final_g=final_g)
    te = MOE_TILE
    n_tiles = -(-2 * n // te) + N_EXPERTS
    rec, rows, cnt, xt = _router_rank(x, lw["norm_ffn"], rw, tm)
    counts = cnt[0, :N_EXPERTS].astype(jnp.int32)
    pos, tile_expert, n_used, last = _route_plan(counts, rows, te, n_tiles, DISPATCH_CHUNK)
    xs = _dispatch(xt, pos, last, n_used, te, n_tiles, DISPATCH_CHUNK)
    y = _grouped_experts(xs, wg, wu, wd, tile_expert, n_used, te, n_tiles)
    return _gather_combine(x, y, rec, pos, tm, final_g=final_g)


def _chunk_experts(wg, wu, wd):
    e, d, f = wg.shape
    c = -(-f // MOE_CHUNK)
    fc = MOE_CHUNK

    def body(wg_ref, wu_ref, wd_ref, og_ref, ou_ref, od_ref):
        valid = f - pl.program_id(1) * fc
        col_ok = lax.broadcasted_iota(jnp.int32, (d, fc), 1) < valid
        row_ok = lax.broadcasted_iota(jnp.int32, (fc, d), 0) < valid
        og_ref[...] = jnp.where(col_ok, wg_ref[...], 0.0).astype(BF16)
        ou_ref[...] = jnp.where(col_ok, wu_ref[...], 0.0).astype(BF16)
        od_ref[...] = jnp.where(row_ok, wd_ref[...], 0.0).astype(BF16)

    cols_in = pl.BlockSpec((None, d, fc), lambda i, j: (i, 0, j))
    rows_in = pl.BlockSpec((None, fc, d), lambda i, j: (i, j, 0))
    cols_out = pl.BlockSpec((None, None, d, fc), lambda i, j: (i, j, 0, 0))
    rows_out = pl.BlockSpec((None, None, fc, d), lambda i, j: (i, j, 0, 0))
    return pl.pallas_call(
        body,
        grid=(e, c),
        in_specs=[cols_in, cols_in, rows_in],
        out_specs=[cols_out, cols_out, rows_out],
        out_shape=[jax.ShapeDtypeStruct((e, c, d, fc), BF16),
                   jax.ShapeDtypeStruct((e, c, d, fc), BF16),
                   jax.ShapeDtypeStruct((e, c, fc, d), BF16)],
        compiler_params=_cparams(("parallel", "parallel")),
        name="chunk_experts",
    )(wg, wu, wd)


def kernel(x_prompt, x_sample, mem_prompt, state_pool, state_conv, state_h, cache_mem_k, cache_mem_v,
           norm_mix, w_in, pool_w, pool_scale, conv_w, conv_b, gate_a_w, gate_a_b, gate_x_w, gate_x_b,
           rglru_lambda, out_norm_pool, out_norm_rnn, w_out, norm_mem, mem_norm, w_mq, w_mk, w_mv, w_mo,
           norm_ffn, ffn_w_gate, ffn_w_up, ffn_w_down, router_w, moe_w_gate, moe_w_up, moe_w_down,
           final_norm):
    P = dict(norm_mix=norm_mix, w_in=w_in, pool_w=pool_w, pool_scale=pool_scale, conv_w=conv_w,
             conv_b=conv_b, gate_a_w=gate_a_w, gate_a_b=gate_a_b, gate_x_w=gate_x_w,
             gate_x_b=gate_x_b, rglru_lambda=rglru_lambda, out_norm_pool=out_norm_pool,
             out_norm_rnn=out_norm_rnn, w_out=w_out, norm_mem=norm_mem, w_mq=w_mq, w_mo=w_mo,
             norm_ffn=norm_ffn, ffn_w_gate=ffn_w_gate, ffn_w_up=ffn_w_up, ffn_w_down=ffn_w_down,
             router_w=router_w, moe_w_gate=moe_w_gate, moe_w_up=moe_w_up, moe_w_down=moe_w_down)
    bsz, t_len, d = x_prompt.shape
    n_dec = x_sample.shape[0]
    n_tok = bsz * t_len
    fg = final_norm.reshape(1, d)
    TM, TT, TM_DEC, BB = 512, 512, n_dec, SUBLANES

    xp = x_prompt
    xs = x_sample.reshape(n_dec, d)
    mem2 = mem_prompt.reshape(bsz * N_MEM, d)
    pools_p, convs_p, hs_p, pools_s, convs_s, hs_s = ([] for _ in range(6))
    mem_k, mem_v, mem_kh, mem_vh = _mem_kv(mem2, mem_norm.reshape(DEPTH, 1, d), w_mk, w_mv, TM)
    mem_k = mem_k.reshape(DEPTH, bsz, N_MEM, d)
    mem_v = mem_v.reshape(DEPTH, bsz, N_MEM, d)
    zeros_hist = jnp.zeros((bsz, POOL_PAD, D_POOL), F32)
    zeros_conv = jnp.zeros((bsz, CONV_PAD, D_RNN), F32)
    zeros_h = jnp.zeros((bsz, 1, D_RNN), F32)

    for l in range(DEPTH):
        lw = _layer_weights(l, P)
        last_g = fg if l == DEPTH - 1 else None

        x2, h_last, pool_tail, conv_tail = _mixer_prompt(xp, zeros_hist, zeros_conv, zeros_h, lw,
                                                         0, TT)
        pools_p.append(pool_tail[:, POOL_PAD - POOL_HIST:, :])
        convs_p.append(conv_tail[:, CONV_PAD - (CONV_WIDTH - 1):, :])
        hs_p.append(h_last[:, 0, :])
        x3 = _attn_prompt(x2, mem_k, mem_v, l, lw["norm_mem"], lw["w_mq"], lw["w_mo"], TT)
        xp = _channel_mixer(x3.reshape(n_tok, d), l, lw, P, TM, last_g, True)
        xp = xp.reshape(bsz, t_len, d)

        proj_s = _norm_matmul(xs, lw["norm_mix"], lw["w_in"], TM_DEC)
        mixed_s, h_new = _mixer_decode(proj_s, jnp.swapaxes(state_pool[l], 0, 1),
                                       jnp.swapaxes(state_conv[l], 0, 1), state_h[l], lw,
                                       PAST_LEN)
        pools_s.append(jnp.concatenate([state_pool[l][:, 1:], proj_s[:, None, :D_POOL]], axis=1))
        convs_s.append(jnp.concatenate([state_conv[l][:, 1:], proj_s[:, None, D_POOL:D_MIX]],
                                       axis=1))
        hs_s.append(h_new)
        xs2 = _matmul_res(mixed_s, lw["w_out"], xs, TM_DEC)
        q_s = _norm_matmul(xs2, lw["norm_mem"], lw["w_mq"], TM_DEC)
        att = _attn_decode_mxu(q_s.reshape(n_dec, MEM_HEADS, MEM_HEAD_DIM), cache_mem_k, cache_mem_v,
                           l, BB)
        xs3 = _matmul_res(att.reshape(n_dec, d), lw["w_mo"], xs2, TM_DEC)
        xs = _channel_mixer(xs3, l, lw, P, TM_DEC, last_g, False)

    kv_shape = (DEPTH, bsz, N_MEM, MEM_HEADS, MEM_HEAD_DIM)
    return (xp, xs.reshape(n_dec, 1, d), jnp.stack(pools_p), jnp.stack(convs_p), jnp.stack(hs_p),
            mem_kh.reshape(kv_shape), mem_vh.reshape(kv_shape), jnp.stack(pools_s),
            jnp.stack(convs_s), jnp.stack(hs_s))
```
